```python
import jax, jax.numpy as jnp
from jax import lax
import numpy as np

D_MODEL = 4096
BATCH = 2
SEQ = 8192
DEPTH = 4
DEC_BATCH = 8
DEC_SEQ = 64
PAST_LEN = 4096

CHUNK = 64
QBLOCK = 128
N_HEADS = 32
QK_NOPE = 128
QK_ROPE = 64
V_DIM = 128
Q_RANK = 1024
KV_RANK = 512
ROPE_THETA = 10000.0
SM_SCALE = (QK_NOPE + QK_ROPE) ** -0.5
D_RNN = D_MODEL
RG_BLOCKS = 16
RG_BW = D_RNN // RG_BLOCKS
CONV_W = 4
RG_C = 8.0
D_FF = 4 * D_MODEL
N_BRANCH = 2
N_MOD = 6
ALPHA = (2 * DEPTH) ** 0.25
BETA = (8 * DEPTH) ** -0.25
OFF_Q = 0
OFF_KV = OFF_Q + Q_RANK
OFF_KR = OFF_KV + KV_RANK
OFF_RX = OFF_KR + QK_ROPE
OFF_RY = OFF_RX + D_RNN
OFF_G = OFF_RY + D_RNN
IN_COLS = OFF_G + N_BRANCH * D_MODEL

kernel_name = 'hybrid_mla_rglru_deepnorm_adaln_stream_step'


def rms_norm(x, g, eps=1e-6):
    xf = x.astype(jnp.float32)
    y = xf * lax.rsqrt(jnp.mean(xf * xf, axis=-1, keepdims=True) + eps)
    return (y * g.astype(jnp.float32)).astype(x.dtype)


def layer_norm(x, g, b, eps=1e-5):
    xf = x.astype(jnp.float32)
    mu = jnp.mean(xf, axis=-1, keepdims=True)
    var = jnp.mean(jnp.square(xf - mu), axis=-1, keepdims=True)
    y = (xf - mu) * lax.rsqrt(var + eps) * g.astype(jnp.float32) + b.astype(jnp.float32)
    return y.astype(x.dtype)


def rope_angles(pos):
    half = QK_ROPE // 2
    inv = ROPE_THETA ** (-jnp.arange(half, dtype=jnp.float32) / half)
    ang = pos.astype(jnp.float32)[:, None] * inv[None, :]
    return jnp.cos(ang), jnp.sin(ang)


def apply_rope(x, cos, sin):
    half = QK_ROPE // 2
    xf = x.astype(jnp.float32)
    x1, x2 = xf[..., :half], xf[..., half:]
    return jnp.concatenate([x1 * cos - x2 * sin, x2 * cos + x1 * sin], axis=-1).astype(x.dtype)


def mla_attention(q_lat, q_rope, ckv, krope, q_pos, k_pos):
    B, S, H, C = q_lat.shape
    qb = QBLOCK if S % QBLOCK == 0 else S
    nb = S // qb
    k_chunk = k_pos // CHUNK

    def one_block(args):
        ql, qr, qp = args
        s = (jnp.einsum('bqhc,bkc->bhqk', ql, ckv).astype(jnp.float32)
             + jnp.einsum('bqhr,bkr->bhqk', qr, krope).astype(jnp.float32)) * SM_SCALE
        mask = k_chunk[None, :] <= (qp // CHUNK)[:, None]
        s = jnp.where(mask[None, None], s, jnp.finfo(jnp.float32).min)
        pr = jax.nn.softmax(s, axis=-1).astype(ckv.dtype)
        return jnp.einsum('bhqk,bkc->bqhc', pr, ckv)

    ql = q_lat.reshape(B, nb, qb, H, C).swapaxes(0, 1)
    qr = q_rope.reshape(B, nb, qb, H, QK_ROPE).swapaxes(0, 1)
    qp = q_pos.reshape(nb, qb)
    out = lax.map(one_block, (ql, qr, qp))
    return out.swapaxes(0, 1).reshape(B, S, H, C)


def rg_lru(xc, h0, w_a, b_a, w_x, b_x, lam):
    B, S, _ = xc.shape
    xf = xc.astype(jnp.float32)
    xb = xf.reshape(B, S, RG_BLOCKS, RG_BW)
    r = jax.nn.sigmoid(jnp.einsum('bsnk,nkj->bsnj', xb, w_a.astype(jnp.float32))
                       + b_a.astype(jnp.float32)).reshape(B, S, D_RNN)
    i = jax.nn.sigmoid(jnp.einsum('bsnk,nkj->bsnj', xb, w_x.astype(jnp.float32))
                       + b_x.astype(jnp.float32)).reshape(B, S, D_RNN)
    log_a = -RG_C * r * jax.nn.softplus(-lam.astype(jnp.float32))
    a = jnp.exp(log_a)
    gain = jnp.sqrt(-jnp.expm1(2.0 * log_a))
    b = gain * (i * xf)
    b = b.at[:, 0].add(a[:, 0] * h0.astype(jnp.float32))

    def combine(e1, e2):
        a1, b1 = e1
        a2, b2 = e2
        return a1 * a2, a2 * b1 + b2

    _, hs = lax.associative_scan(combine, (a, b), axis=1)
    return hs.astype(xc.dtype), hs[:, -1].astype(xc.dtype)


def token_mixers(h, ckv_past, kr_past, h_past, conv_past, p, l):
    B, S, _ = h.shape
    P = ckv_past.shape[1]
    proj = h @ p['w_in'][l]
    q_c = rms_norm(proj[..., OFF_Q:OFF_KV], p['q_norm_g'][l])
    ckv_new = rms_norm(proj[..., OFF_KV:OFF_KR], p['kv_norm_g'][l])
    kr_raw = proj[..., OFF_KR:OFF_RX]
    xr = proj[..., OFF_RX:OFF_RY]
    yr = proj[..., OFF_RY:OFF_G]
    gates = jax.nn.sigmoid(proj[..., OFF_G:].astype(jnp.float32)).astype(h.dtype).reshape(B, S, N_BRANCH, D_MODEL)

    q_pos = P + jnp.arange(S, dtype=jnp.int32)
    cos, sin = rope_angles(q_pos)
    q = (q_c @ p['w_uq'][l]).reshape(B, S, N_HEADS, QK_NOPE + QK_ROPE)
    q_nope = q[..., :QK_NOPE]
    q_rope = apply_rope(q[..., QK_NOPE:], cos[:, None, :], sin[:, None, :])
    kr_new = apply_rope(kr_raw, cos, sin)
    q_lat = jnp.einsum('bshd,chd->bshc', q_nope, p['w_uk'][l])
    ckv = jnp.concatenate([ckv_past, ckv_new], axis=1)
    krope = jnp.concatenate([kr_past, kr_new], axis=1)
    k_pos = jnp.arange(P + S, dtype=jnp.int32)
    o_lat = mla_attention(q_lat, q_rope, ckv, krope, q_pos, k_pos)
    o = jnp.einsum('bshc,chv->bshv', o_lat, p['w_uv'][l]).reshape(B, S, N_HEADS * V_DIM)
    branch_a = o @ p['w_o_attn'][l]

    xpad = jnp.concatenate([conv_past, xr], axis=1)
    cw = p['conv_w'][l]
    xc = p['conv_b'][l] + sum(xpad[:, k:k + S] * cw[k] for k in range(CONV_W))
    hs, h_last = rg_lru(xc, h_past, p['w_rg_a'][l], p['b_rg_a'][l], p['w_rg_x'][l], p['b_rg_x'][l], p['rg_lambda'][l])
    branch_b = (hs * jax.nn.gelu(yr)) @ p['w_o_rnn'][l]

    merged = gates[:, :, 0] * branch_a + gates[:, :, 1] * branch_b
    out = merged @ p['w_out'][l]
    return out, ckv_new, kr_new, h_last, xpad[:, -(CONV_W - 1):]


def run_trunk(x, cond, ckv_past, kr_past, h_past, conv_past, p):
    B = x.shape[0]
    x = layer_norm(x, p['ln_in_g'], p['ln_in_b'])
    cm = jax.nn.silu(cond.astype(jnp.float32)).astype(cond.dtype)
    ckv_l, kr_l, h_l, conv_l = [], [], [], []
    for l in range(DEPTH):
        mod = (cm @ p['w_mod'][l] + p['b_mod'][l]).reshape(B, N_MOD, D_MODEL)
        sh_m, sc_m, g_m = mod[:, 0, None], mod[:, 1, None], mod[:, 2, None]
        sh_f, sc_f, g_f = mod[:, 3, None], mod[:, 4, None], mod[:, 5, None]
        h = x * (1.0 + sc_m) + sh_m
        mix, ckv_n, kr_n, h_n, conv_n = token_mixers(h, ckv_past[l], kr_past[l], h_past[l], conv_past[l], p, l)
        x = layer_norm(ALPHA * x + (1.0 + g_m) * mix, p['ln_mix_g'][l], p['ln_mix_b'][l])
        h = x * (1.0 + sc_f) + sh_f
        ff = jnp.square(jax.nn.relu(h @ p['w_up'][l])) @ p['w_down'][l]
        x = layer_norm(ALPHA * x + (1.0 + g_f) * ff, p['ln_ffn_g'][l], p['ln_ffn_b'][l])
        ckv_l.append(ckv_n)
        kr_l.append(kr_n)
        h_l.append(h_n)
        conv_l.append(conv_n)
    return x, jnp.stack(ckv_l), jnp.stack(kr_l), jnp.stack(h_l), jnp.stack(conv_l)


def setup_inputs(seed: int = 0) -> dict:
    key = jax.random.key(seed)
    ks = iter(jax.random.split(key, 48))
    f32 = jnp.float32

    def nrm(shape, scale):
        return jax.random.normal(next(ks), shape, f32) * scale

    def gain(shape):
        return 1.0 + nrm(shape, 0.01)

    u = jax.random.uniform(next(ks), (DEPTH, D_RNN), f32, minval=0.9, maxval=0.999)
    rg_lambda = jnp.log(u) - jnp.log1p(-u)
    return {
        'x_prompt': nrm((BATCH, SEQ, D_MODEL), 1.0),
        'x_sample': nrm((DEC_BATCH, DEC_SEQ, D_MODEL), 1.0),
        'cache_kv_latent': nrm((DEPTH, DEC_BATCH, PAST_LEN, KV_RANK), 1.0),
        'cache_k_rope': nrm((DEPTH, DEC_BATCH, PAST_LEN, QK_ROPE), 1.0),
        'state_rglru_h': nrm((DEPTH, DEC_BATCH, D_RNN), 0.5),
        'state_conv': nrm((DEPTH, DEC_BATCH, CONV_W - 1, D_RNN), 1.0),
        'c_prompt': nrm((BATCH, D_MODEL), 1.0),
        'c_sample': nrm((DEC_BATCH, D_MODEL), 1.0),
        'ln_in_g': gain((D_MODEL,)),
        'ln_in_b': nrm((D_MODEL,), 0.01),
        'w_in': nrm((DEPTH, D_MODEL, IN_COLS), D_MODEL ** -0.5),
        'q_norm_g': gain((DEPTH, Q_RANK)),
        'w_uq': nrm((DEPTH, Q_RANK, N_HEADS * (QK_NOPE + QK_ROPE)), Q_RANK ** -0.5),
        'kv_norm_g': gain((DEPTH, KV_RANK)),
        'w_uk': nrm((DEPTH, KV_RANK, N_HEADS, QK_NOPE), KV_RANK ** -0.5),
        'w_uv': nrm((DEPTH, KV_RANK, N_HEADS, V_DIM), KV_RANK ** -0.5),
        'w_o_attn': nrm((DEPTH, N_HEADS * V_DIM, D_MODEL), (N_HEADS * V_DIM) ** -0.5),
        'conv_w': nrm((DEPTH, CONV_W, D_RNN), CONV_W ** -0.5),
        'conv_b': nrm((DEPTH, D_RNN), 0.01),
        'w_rg_a': nrm((DEPTH, RG_BLOCKS, RG_BW, RG_BW), RG_BW ** -0.5),
        'b_rg_a': nrm((DEPTH, RG_BLOCKS, RG_BW), 0.01),
        'w_rg_x': nrm((DEPTH, RG_BLOCKS, RG_BW, RG_BW), RG_BW ** -0.5),
        'b_rg_x': nrm((DEPTH, RG_BLOCKS, RG_BW), 0.01),
        'rg_lambda': rg_lambda,
        'w_o_rnn': nrm((DEPTH, D_RNN, D_MODEL), D_RNN ** -0.5),
        'w_out': nrm((DEPTH, D_MODEL, D_MODEL), BETA * D_MODEL ** -0.5),
        'w_mod': nrm((DEPTH, D_MODEL, N_MOD * D_MODEL), 0.2 * D_MODEL ** -0.5),
        'b_mod': nrm((DEPTH, N_MOD * D_MODEL), 0.01),
        'ln_mix_g': gain((DEPTH, D_MODEL)),
        'ln_mix_b': nrm((DEPTH, D_MODEL), 0.01),
        'w_up': nrm((DEPTH, D_MODEL, D_FF), D_MODEL ** -0.5),
        'w_down': nrm((DEPTH, D_FF, D_MODEL), BETA * D_FF ** -0.5),
        'ln_ffn_g': gain((DEPTH, D_MODEL)),
        'ln_ffn_b': nrm((DEPTH, D_MODEL), 0.01),
    }


def reference(x_prompt, x_sample, cache_kv_latent, cache_k_rope, state_rglru_h, state_conv,
              c_prompt, c_sample, ln_in_g, ln_in_b, w_in, q_norm_g, w_uq, kv_norm_g, w_uk, w_uv,
              w_o_attn, conv_w, conv_b, w_rg_a, b_rg_a, w_rg_x, b_rg_x, rg_lambda, w_o_rnn, w_out,
              w_mod, b_mod, ln_mix_g, ln_mix_b, w_up, w_down, ln_ffn_g, ln_ffn_b):
    p = {
        'ln_in_g': ln_in_g, 'ln_in_b': ln_in_b, 'w_in': w_in, 'q_norm_g': q_norm_g, 'w_uq': w_uq,
        'kv_norm_g': kv_norm_g, 'w_uk': w_uk, 'w_uv': w_uv, 'w_o_attn': w_o_attn,
        'conv_w': conv_w, 'conv_b': conv_b, 'w_rg_a': w_rg_a, 'b_rg_a': b_rg_a,
        'w_rg_x': w_rg_x, 'b_rg_x': b_rg_x, 'rg_lambda': rg_lambda, 'w_o_rnn': w_o_rnn,
        'w_out': w_out, 'w_mod': w_mod, 'b_mod': b_mod, 'ln_mix_g': ln_mix_g, 'ln_mix_b': ln_mix_b,
        'w_up': w_up, 'w_down': w_down, 'ln_ffn_g': ln_ffn_g, 'ln_ffn_b': ln_ffn_b,
    }
    B = x_prompt.shape[0]
    dt = x_prompt.dtype
    y_prompt, ckv_p, kr_p, h_p, conv_p = run_trunk(
        x_prompt, c_prompt,
        jnp.zeros((DEPTH, B, 0, KV_RANK), dt), jnp.zeros((DEPTH, B, 0, QK_ROPE), dt),
        jnp.zeros((DEPTH, B, D_RNN), dt), jnp.zeros((DEPTH, B, CONV_W - 1, D_RNN), dt), p)
    y_sample, ckv_s, kr_s, h_s, conv_s = run_trunk(
        x_sample, c_sample, cache_kv_latent, cache_k_rope, state_rglru_h, state_conv, p)
    return (y_prompt, y_sample, ckv_p, kr_p, h_p, conv_p, ckv_s, kr_s, h_s, conv_s)
```

```python
import functools
import math

import jax
import jax.numpy as jnp
from jax import lax
from jax.experimental import pallas as pl
from jax.experimental.pallas import tpu as pltpu

F32 = jnp.float32
BF16 = jnp.bfloat16

CHUNK = 64
ROPE_THETA = 10000.0
RG_C = 8.0
LN_EPS = 1e-5
RMS_EPS = 1e-6

LANES = 128
SUBLANES = 8
HEAD_W = 2 * LANES
VMEM_LIMIT_BYTES = 56 * 1024 * 1024


def _cparams(*sem):
    return pltpu.CompilerParams(dimension_semantics=sem, vmem_limit_bytes=VMEM_LIMIT_BYTES)


def _tile(n, pref):
    if n <= pref:
        return n
    t = pref
    while n % t:
        t //= 2
    return t


def _mod_kernel(c_ref, w_ref, b_ref, o_ref):
    c = c_ref[...]
    cm = (c * jax.nn.sigmoid(c)).astype(BF16)
    o_ref[0] = jnp.dot(cm, w_ref[0].astype(BF16), preferred_element_type=F32) + b_ref[0]


def _mod_call(c_all, w_mod, b_mod):
    depth, d, n = w_mod.shape
    rows = c_all.shape[0]
    tn = _tile(n, 512)
    return pl.pallas_call(
        _mod_kernel,
        grid=(depth, n // tn),
        in_specs=[
            pl.BlockSpec((rows, d), lambda l, j: (0, 0)),
            pl.BlockSpec((1, d, tn), lambda l, j: (l, 0, j)),
            pl.BlockSpec((1, 1, tn), lambda l, j: (l, 0, j)),
        ],
        out_specs=pl.BlockSpec((1, rows, tn), lambda l, j: (l, 0, j)),
        out_shape=jax.ShapeDtypeStruct((depth, rows, n), F32),
        compiler_params=_cparams("parallel", "arbitrary"),
        name="mod",
    )(c_all, w_mod, b_mod.reshape(depth, 1, n))


def _ln_mod_kernel(*refs, has_res, has_mod, gate_row, sc_row, sh_row, alpha):
    it = iter(refs)
    x_ref = next(it)
    if has_res:
        y_ref = next(it)
        gmod_ref = next(it)
    g_ref = next(it)
    b_ref = next(it)
    if has_mod:
        nmod_ref = next(it)
    xo_ref = next(it)
    if has_mod:
        h_ref = next(it)

    x = x_ref[...]
    if has_res:
        gate = gmod_ref[0, :, gate_row:gate_row + 1, :]
        x = alpha * x + (1.0 + gate) * y_ref[...]
    mu = jnp.mean(x, axis=-1, keepdims=True)
    xc = x - mu
    var = jnp.mean(xc * xc, axis=-1, keepdims=True)
    y = xc * lax.rsqrt(var + LN_EPS) * g_ref[...] + b_ref[...]
    xo_ref[...] = y
    if has_mod:
        sc = nmod_ref[0, :, sc_row:sc_row + 1, :]
        sh = nmod_ref[0, :, sh_row:sh_row + 1, :]
        h_ref[...] = (y * (1.0 + sc) + sh).astype(BF16)


def _ln_mod_call(x, ln_g, ln_b, *, res=None, gmod=None, gate_layer=0, gate_row=0,
                 nmod=None, mod_layer=0, sc_row=0, sh_row=0, alpha=1.0):
    b, s, d = x.shape
    rows = _tile(s, 256)
    nb = _tile(b, max(1, 256 // rows))
    grid = (b // nb, s // rows)
    blk = pl.BlockSpec((nb, rows, d), lambda i, j: (i, j, 0))
    vec = pl.BlockSpec((1, d), lambda i, j: (0, 0))
    has_res = res is not None
    has_mod = nmod is not None
    args, specs = [x], [blk]
    if has_res:
        n_mod = gmod.shape[2]
        args += [res, gmod]
        specs += [blk, pl.BlockSpec((1, nb, n_mod, d), lambda i, j: (gate_layer, i, 0, 0))]
    args += [ln_g.reshape(1, d), ln_b.reshape(1, d)]
    specs += [vec, vec]
    out_shape = [jax.ShapeDtypeStruct((b, s, d), F32)]
    out_specs = [blk]
    if has_mod:
        n_mod = nmod.shape[2]
        args += [nmod]
        specs += [pl.BlockSpec((1, nb, n_mod, d), lambda i, j: (mod_layer, i, 0, 0))]
        out_shape += [jax.ShapeDtypeStruct((b, s, d), BF16)]
        out_specs += [blk]
    kern = functools.partial(_ln_mod_kernel, has_res=has_res, has_mod=has_mod, gate_row=gate_row,
                             sc_row=sc_row, sh_row=sh_row, alpha=alpha)
    outs = pl.pallas_call(
        kern, grid=grid, in_specs=specs, out_specs=out_specs, out_shape=out_shape,
        compiler_params=_cparams("parallel", "parallel"), name="ln_mod",
    )(*args)
    return outs if has_mod else (outs[0], None)


def _mm_kernel(x_ref, w_ref, *rest, nk, n_extra, epilogue):
    extra = rest[:n_extra]
    o_ref = rest[n_extra]
    if nk == 1:
        acc = jnp.dot(x_ref[...], w_ref[...], preferred_element_type=F32)
        epilogue(acc, extra, o_ref)
    else:
        acc_ref = rest[n_extra + 1]
        k = pl.program_id(2)

        @pl.when(k == 0)
        def _():
            acc_ref[...] = jnp.zeros_like(acc_ref)

        acc_ref[...] += jnp.dot(x_ref[...], w_ref[...], preferred_element_type=F32)

        @pl.when(k == nk - 1)
        def _():
            epilogue(acc_ref[...], extra, o_ref)


def _matmul(x, w, layer, *, out_dtype, epilogue, tm=1024, tn=512, tk=None, extra=(), name="matmul"):
    m, kdim = x.shape
    n = w.shape[2]
    tm = _tile(m, tm)
    tn = _tile(n, tn)
    tk = kdim if tk is None else _tile(kdim, tk)
    nk = kdim // tk
    in_specs = [
        pl.BlockSpec((tm, tk), lambda i, j, k: (i, k)),
        pl.BlockSpec((None, tk, tn), lambda i, j, k: (layer, k, j)),
    ]
    args = [x, w]
    for arr, bshape, imap in extra:
        args.append(arr)
        in_specs.append(pl.BlockSpec(bshape, imap))
    scratch = [pltpu.VMEM((tm, tn), F32)] if nk > 1 else []
    kern = functools.partial(_mm_kernel, nk=nk, n_extra=len(extra), epilogue=epilogue)
    return pl.pallas_call(
        kern,
        grid=(m // tm, n // tn, nk),
        in_specs=in_specs,
        out_specs=pl.BlockSpec((tm, tn), lambda i, j, k: (i, j)),
        out_shape=jax.ShapeDtypeStruct((m, n), out_dtype),
        scratch_shapes=scratch,
        compiler_params=_cparams("parallel", "arbitrary", "arbitrary"),
        name=name,
    )(*args)


def _ep_store(acc, extra, o_ref):
    o_ref[...] = acc.astype(o_ref.dtype)


def _ep_relu2(acc, extra, o_ref):
    r = jnp.maximum(acc, 0.0)
    o_ref[...] = (r * r).astype(o_ref.dtype)


def _gelu_tanh(x):
    c = math.sqrt(2.0 / math.pi)
    return x * (0.5 * (1.0 + jnp.tanh(c * (x + 0.044715 * (x * x * x)))))


def _ep_inproj_b(acc, extra, o_ref, *, n_raw, n_gelu):
    j = pl.program_id(1)

    @pl.when(j < n_raw)
    def _():
        o_ref[...] = acc.astype(o_ref.dtype)

    @pl.when((j >= n_raw) & (j < n_raw + n_gelu))
    def _():
        o_ref[...] = _gelu_tanh(acc).astype(o_ref.dtype)

    @pl.when(j >= n_raw + n_gelu)
    def _():
        o_ref[...] = jax.nn.sigmoid(acc).astype(o_ref.dtype)


def _rope_slab(hi, cos, sin, rope):
    return hi * cos + pltpu.roll(hi, rope, axis=1) * sin


def _ep_uq(acc, extra, o_ref, *, heads, rope, scale):
    cos = extra[0][...]
    sin = extra[1][...]
    for hh in range(heads):
        base = hh * HEAD_W
        lo = acc[:, base:base + LANES] * scale
        hi = _rope_slab(acc[:, base + LANES:base + HEAD_W], cos, sin, rope) * scale
        o_ref[:, base:base + LANES] = lo.astype(o_ref.dtype)
        o_ref[:, base + LANES:base + HEAD_W] = hi.astype(o_ref.dtype)


def _inproj_a_kernel(h_ref, w_ref, qg_ref, kvg_ref, cos_ref, sin_ref,
                     qc_ref, ckv_ref, kr_ref, ckr_ref, *, q_rank, kv_rank, rope):
    acc = jnp.dot(h_ref[...], w_ref[...], preferred_element_type=F32)
    q = acc[:, :q_rank]
    qn = q * lax.rsqrt(jnp.mean(q * q, axis=-1, keepdims=True) + RMS_EPS) * qg_ref[...]
    qc_ref[...] = qn.astype(BF16)
    kv = acc[:, q_rank:q_rank + kv_rank]
    kvn = kv * lax.rsqrt(jnp.mean(kv * kv, axis=-1, keepdims=True) + RMS_EPS) * kvg_ref[...]
    ckv_ref[...] = kvn
    kr = _rope_slab(acc[:, q_rank + kv_rank:], cos_ref[...], sin_ref[...], rope)
    kr_ref[...] = kr[:, :rope]
    ckr_ref[:, :kv_rank] = kvn.astype(BF16)
    ckr_ref[:, kv_rank:] = kr.astype(BF16)


def _inproj_a_call(h, w_a, layer, qg, kvg, cos_t, sin_t, *, q_rank, kv_rank, rope, tm):
    m, d = h.shape
    na = w_a.shape[2]
    nt = cos_t.shape[0] // tm
    row = lambda i: (i, 0)
    fixed = lambda i: (0, 0)
    tab = lambda i: (i % nt, 0)
    kern = functools.partial(_inproj_a_kernel, q_rank=q_rank, kv_rank=kv_rank, rope=rope)
    return pl.pallas_call(
        kern,
        grid=(m // tm,),
        in_specs=[
            pl.BlockSpec((tm, d), row),
            pl.BlockSpec((None, d, na), lambda i: (layer, 0, 0), pipeline_mode=pl.Buffered(1)),
            pl.BlockSpec((1, q_rank), fixed),
            pl.BlockSpec((1, kv_rank), fixed),
            pl.BlockSpec((tm, LANES), tab),
            pl.BlockSpec((tm, LANES), tab),
        ],
        out_specs=[
            pl.BlockSpec((tm, q_rank), row),
            pl.BlockSpec((tm, kv_rank), row),
            pl.BlockSpec((tm, rope), row),
            pl.BlockSpec((tm, kv_rank + LANES), row),
        ],
        out_shape=[
            jax.ShapeDtypeStruct((m, q_rank), BF16),
            jax.ShapeDtypeStruct((m, kv_rank), F32),
            jax.ShapeDtypeStruct((m, rope), F32),
            jax.ShapeDtypeStruct((m, kv_rank + LANES), BF16),
        ],
        compiler_params=_cparams("parallel"),
        name="inproj_a",
    )(h, w_a, qg, kvg, cos_t, sin_t)


def _dual_kernel(a_ref, b_ref, wa_ref, wb_ref, ga_ref, gb_ref, o_ref):
    ya = jnp.dot(a_ref[...], wa_ref[...], preferred_element_type=F32)
    yb = jnp.dot(b_ref[...], wb_ref[...], preferred_element_type=F32)
    o_ref[...] = (ga_ref[...].astype(F32) * ya + gb_ref[...].astype(F32) * yb).astype(o_ref.dtype)


def _dual_call(a, b, wa, wb, layer, gates_src, ga_col, gb_col, *, tm=512, tn=512):
    m, ka = a.shape
    kb = b.shape[1]
    n = wa.shape[2]
    tm = _tile(m, tm)
    tn = _tile(n, tn)
    ga_blk, gb_blk = ga_col // tn, gb_col // tn
    return pl.pallas_call(
        _dual_kernel,
        grid=(m // tm, n // tn),
        in_specs=[
            pl.BlockSpec((tm, ka), lambda i, j: (i, 0)),
            pl.BlockSpec((tm, kb), lambda i, j: (i, 0)),
            pl.BlockSpec((None, ka, tn), lambda i, j: (layer, 0, j)),
            pl.BlockSpec((None, kb, tn), lambda i, j: (layer, 0, j)),
            pl.BlockSpec((tm, tn), lambda i, j: (i, ga_blk + j)),
            pl.BlockSpec((tm, tn), lambda i, j: (i, gb_blk + j)),
        ],
        out_specs=pl.BlockSpec((tm, tn), lambda i, j: (i, j)),
        out_shape=jax.ShapeDtypeStruct((m, n), BF16),
        compiler_params=_cparams("parallel", "arbitrary"),
        name="branch_merge",
    )(a, b, wa, wb, gates_src, gates_src)


def _rnn_kernel(xr_ref, gy_ref, cw_ref, cb_ref, wa_ref, wx_ref, ba_ref, bx_ref, lam_ref, h0_ref, cp_ref,
                hb_ref, hl_ref, cn_ref, xbuf, hcar, *, ts, conv_w):
    t = pl.program_id(2)
    pad = SUBLANES
    hist = conv_w - 1

    @pl.when(t == 0)
    def _():
        xbuf[0:pad, :] = jnp.zeros((pad, xbuf.shape[1]), F32)
        xbuf[pad - hist:pad, :] = cp_ref[0, 0]
        hcar[...] = h0_ref[0, 0]

    x = xr_ref[0].astype(F32)
    xbuf[pad:pad + ts, :] = x
    xc = cb_ref[...] + cw_ref[hist:hist + 1, :] * x
    for k in range(hist):
        d = hist - k
        xc = xc + cw_ref[k:k + 1, :] * xbuf[pad - d:pad - d + ts, :]
    cn_ref[0] = xbuf[pad + ts - hist:pad + ts, :]
    xbuf[0:pad, :] = xbuf[ts:ts + pad, :]

    xb = xc.astype(BF16)
    r = jax.nn.sigmoid(jnp.dot(xb, wa_ref[0], preferred_element_type=F32) + ba_ref[...])
    ig = jax.nn.sigmoid(jnp.dot(xb, wx_ref[0], preferred_element_type=F32) + bx_ref[...])
    nl = -lam_ref[...]
    softplus = jnp.maximum(nl, 0.0) + jnp.log1p(jnp.exp(-jnp.abs(nl)))
    log_a = (-RG_C) * r * softplus
    a = jnp.exp(log_a)
    gain = jnp.sqrt(-jnp.tanh(log_a) * (a * a + 1.0))
    bv = gain * (ig * xc)

    row = lax.broadcasted_iota(jnp.int32, a.shape, 0)
    d = 1
    while d < ts:
        a_sh = pltpu.roll(a, d, axis=0)
        b_sh = pltpu.roll(bv, d, axis=0)
        keep = row >= d
        bv = jnp.where(keep, a * b_sh + bv, bv)
        a = jnp.where(keep, a * a_sh, a)
        d *= 2
    h = a * hcar[...] + bv
    h_last = h[ts - 1:ts, :]
    hcar[...] = h_last
    hl_ref[0] = h_last
    hb_ref[0] = (h * gy_ref[0].astype(F32)).astype(BF16)


def _rnn_call(proj_b, conv_w_l, conv_b_l, wa, wx, wlayer, ba, bx, lam, h0, cpast, layer, *, d_rnn, xr_col, gy_col):
    b, s, _ = proj_b.shape
    _, nblk, bw, _ = wa.shape
    conv_w = conv_w_l.shape[0]
    hist = conv_w - 1
    assert s >= hist and s % SUBLANES == 0
    ts = _tile(s, 512)
    xr_blk, gy_blk = xr_col // bw, gy_col // bw
    vec = lambda bi, n, t: (0, n)
    kern = functools.partial(_rnn_kernel, ts=ts, conv_w=conv_w)
    return pl.pallas_call(
        kern,
        grid=(b, nblk, s // ts),
        in_specs=[
            pl.BlockSpec((1, ts, bw), lambda bi, n, t: (bi, t, xr_blk + n)),
            pl.BlockSpec((1, ts, bw), lambda bi, n, t: (bi, t, gy_blk + n)),
            pl.BlockSpec((conv_w, bw), vec),
            pl.BlockSpec((1, bw), vec),
            pl.BlockSpec((None, 1, bw, bw), lambda bi, n, t: (wlayer, n, 0, 0)),
            pl.BlockSpec((None, 1, bw, bw), lambda bi, n, t: (wlayer, n, 0, 0)),
            pl.BlockSpec((1, bw), vec),
            pl.BlockSpec((1, bw), vec),
            pl.BlockSpec((1, bw), vec),
            pl.BlockSpec((1, 1, 1, bw), lambda bi, n, t: (layer, bi, 0, n)),
            pl.BlockSpec((1, 1, hist, bw), lambda bi, n, t: (layer, bi, 0, n)),
        ],
        out_specs=[
            pl.BlockSpec((1, ts, bw), lambda bi, n, t: (bi, t, n)),
            pl.BlockSpec((1, 1, bw), lambda bi, n, t: (bi, 0, n)),
            pl.BlockSpec((1, hist, bw), lambda bi, n, t: (bi, 0, n)),
        ],
        out_shape=[
            jax.ShapeDtypeStruct((b, s, d_rnn), BF16),
            jax.ShapeDtypeStruct((b, 1, d_rnn), F32),
            jax.ShapeDtypeStruct((b, hist, d_rnn), F32),
        ],
        scratch_shapes=[pltpu.VMEM((SUBLANES + ts, bw), F32), pltpu.VMEM((1, bw), F32)],
        compiler_params=_cparams("parallel", "parallel", "arbitrary"),
        name="rglru",
    )(proj_b, proj_b, conv_w_l, conv_b_l.reshape(1, d_rnn), wa, wx, ba.reshape(1, d_rnn),
      bx.reshape(1, d_rnn), lam.reshape(1, d_rnn), h0, cpast)


def _attn_kernel(q_ref, k_ref, v_ref, o_ref, m_sc, l_sc, acc_sc, *, tq):
    i = pl.program_id(2)
    q = q_ref[0]
    m_sc[...] = jnp.full(m_sc.shape, -jnp.inf, F32)
    l_sc[...] = jnp.zeros(l_sc.shape, F32)
    acc_sc[...] = jnp.zeros(acc_sc.shape, F32)

    def step(j, masked):
        off = pl.multiple_of(j * tq, tq)
        k = k_ref[0, pl.ds(off, tq), :]
        v = v_ref[0, pl.ds(off, tq), :]
        s = lax.dot_general(q, k, (((1,), (1,)), ((), ())), preferred_element_type=F32)
        if masked:
            qc = lax.broadcasted_iota(jnp.int32, s.shape, 0) // CHUNK
            kc = lax.broadcasted_iota(jnp.int32, s.shape, 1) // CHUNK
            s = jnp.where(kc <= qc, s, jnp.finfo(F32).min)
        m_prev = m_sc[...]
        m_new = jnp.maximum(m_prev, jnp.max(s, axis=-1, keepdims=True))
        alpha = jnp.exp(m_prev - m_new)
        p = jnp.exp(s - m_new)
        l_sc[...] = alpha * l_sc[...] + jnp.sum(p, axis=-1, keepdims=True)
        acc_sc[...] = alpha * acc_sc[...] + jnp.dot(p.astype(BF16), v, preferred_element_type=F32)
        m_sc[...] = m_new

    def body(j, carry):
        step(j, False)
        return carry

    lax.fori_loop(0, i, body, 0)
    step(i, True)
    o_ref[0] = (acc_sc[...] / l_sc[...]).astype(o_ref.dtype)


def _attn_call(q_aug, kv_aug, *, heads, v_dim):
    b, s, _ = q_aug.shape
    tq = _tile(s, 512)
    assert tq % CHUNK == 0
    v_blk0 = heads * HEAD_W // v_dim
    kern = functools.partial(_attn_kernel, tq=tq)
    return pl.pallas_call(
        kern,
        grid=(b, heads, s // tq),
        in_specs=[
            pl.BlockSpec((1, tq, HEAD_W), lambda bi, h, i: (bi, i, h)),
            pl.BlockSpec((1, s, HEAD_W), lambda bi, h, i: (bi, 0, h)),
            pl.BlockSpec((1, s, v_dim), lambda bi, h, i: (bi, 0, v_blk0 + h)),
        ],
        out_specs=pl.BlockSpec((1, tq, v_dim), lambda bi, h, i: (bi, i, h)),
        out_shape=jax.ShapeDtypeStruct((b, s, heads * v_dim), BF16),
        scratch_shapes=[pltpu.VMEM((tq, 1), F32), pltpu.VMEM((tq, 1), F32), pltpu.VMEM((tq, v_dim), F32)],
        compiler_params=_cparams("parallel", "parallel", "arbitrary"),
        name="attn_prompt",
    )(q_aug, kv_aug, kv_aug)


def _attn_dec_kernel(q_ref, cp_ref, kp_ref, cn_ref, wuk_ref, wuv_ref, o_ref, cpb, kpb, *, hg, kv_rank, rope, s):
    g = pl.program_id(1)

    @pl.when(g == 0)
    def _():
        cpb[...] = cp_ref[0, 0].astype(BF16)
        kpb[...] = kp_ref[0, 0].astype(BF16)

    nt = (((1,), (1,)), ((), ()))
    ql, qr = [], []
    for hh in range(hg):
        base = hh * HEAD_W
        qn = q_ref[0, :, base:base + LANES]
        ql.append(jnp.dot(qn, wuk_ref[hh], preferred_element_type=F32).astype(BF16))
        qr.append(q_ref[0, :, base + LANES:base + LANES + rope])
    ql = jnp.concatenate(ql, axis=0)
    qr = jnp.concatenate(qr, axis=0)
    c_new = cn_ref[0, :, :kv_rank]
    k_new = cn_ref[0, :, kv_rank:kv_rank + rope]
    c_past = cpb[...]
    s_past = (lax.dot_general(ql, c_past, nt, preferred_element_type=F32)
              + lax.dot_general(qr, kpb[...], nt, preferred_element_type=F32))
    s_new = (lax.dot_general(ql, c_new, nt, preferred_element_type=F32)
             + lax.dot_general(qr, k_new, nt, preferred_element_type=F32))
    m = jnp.maximum(jnp.max(s_past, axis=-1, keepdims=True), jnp.max(s_new, axis=-1, keepdims=True))
    p_past = jnp.exp(s_past - m)
    p_new = jnp.exp(s_new - m)
    den = jnp.sum(p_past, axis=-1, keepdims=True) + jnp.sum(p_new, axis=-1, keepdims=True)
    o_lat = (jnp.dot(p_past.astype(BF16), c_past, preferred_element_type=F32)
             + jnp.dot(p_new.astype(BF16), c_new, preferred_element_type=F32)) / den
    o_lat = o_lat.astype(BF16)
    for hh in range(hg):
        oh = jnp.dot(o_lat[hh * s:(hh + 1) * s, :], wuv_ref[hh], preferred_element_type=F32)
        o_ref[0, :, hh * LANES:(hh + 1) * LANES] = oh.astype(o_ref.dtype)


def _attn_dec_call(q_aug, cache_kv, cache_kr, ckr, wuk_t, wuv_h, layer, *, heads, kv_rank, rope, v_dim):
    b, s, _ = q_aug.shape
    past = cache_kv.shape[2]
    assert past % CHUNK == 0 and s <= CHUNK and v_dim == LANES
    hg = _tile(heads, 4)
    kern = functools.partial(_attn_dec_kernel, hg=hg, kv_rank=kv_rank, rope=rope, s=s)
    return pl.pallas_call(
        kern,
        grid=(b, heads // hg),
        in_specs=[
            pl.BlockSpec((1, s, hg * HEAD_W), lambda bi, g: (bi, 0, g)),
            pl.BlockSpec((1, 1, past, kv_rank), lambda bi, g: (layer, bi, 0, 0)),
            pl.BlockSpec((1, 1, past, rope), lambda bi, g: (layer, bi, 0, 0)),
            pl.BlockSpec((1, s, kv_rank + LANES), lambda bi, g: (bi, 0, 0)),
            pl.BlockSpec((None, hg, LANES, kv_rank), lambda bi, g: (layer, g, 0, 0)),
            pl.BlockSpec((None, hg, kv_rank, v_dim), lambda bi, g: (layer, g, 0, 0)),
        ],
        out_specs=pl.BlockSpec((1, s, hg * v_dim), lambda bi, g: (bi, 0, g)),
        out_shape=jax.ShapeDtypeStruct((b, s, heads * v_dim), BF16),
        scratch_shapes=[pltpu.VMEM((past, kv_rank), BF16), pltpu.VMEM((past, rope), BF16)],
        compiler_params=_cparams("parallel", "arbitrary"),
        name="attn_sample",
    )(q_aug, cache_kv, cache_kr, ckr, wuk_t, wuv_h)


def _rotate_half_cols(w, half):
    return jnp.concatenate([-w[..., half:], w[..., :half]], axis=-1)


def _prep_weights(p, dims):
    q_rank, kv_rank, rope, heads, nope, v_dim, d_rnn, d_model = (
        dims[k] for k in ("q_rank", "kv_rank", "rope", "heads", "nope", "v_dim", "d_rnn", "d_model"))
    half = rope // 2
    depth = p["w_in"].shape[0]
    off_kr = q_rank + kv_rank
    off_rx = off_kr + rope
    w_in = p["w_in"]
    w_kr = w_in[:, :, off_kr:off_rx]
    w_a = jnp.concatenate([w_in[:, :, :off_kr], w_kr, _rotate_half_cols(w_kr, half)], axis=-1).astype(BF16)
    w_b = w_in[:, :, off_rx:].astype(BF16)

    wq = p["w_uq"].reshape(depth, q_rank, heads, nope + rope)
    wq_r = wq[..., nope:]
    w_uq = jnp.concatenate([wq[..., :nope], wq_r, _rotate_half_cols(wq_r, half)], axis=-1)
    w_uq = w_uq.reshape(depth, q_rank, heads * HEAD_W).astype(BF16)

    zk = jnp.zeros((depth, kv_rank, heads, HEAD_W - nope), F32)
    k_top = jnp.concatenate([p["w_uk"], zk], axis=-1)
    eye = jnp.zeros((LANES, HEAD_W), F32).at[jnp.arange(rope), nope + jnp.arange(rope)].set(1.0)
    k_bot = jnp.broadcast_to(eye[None, :, None, :], (depth, LANES, heads, HEAD_W))
    k_w = jnp.concatenate([k_top, k_bot], axis=1).reshape(depth, kv_rank + LANES, heads * HEAD_W)
    v_w = jnp.concatenate([p["w_uv"], jnp.zeros((depth, LANES, heads, v_dim), F32)], axis=1)
    v_w = v_w.reshape(depth, kv_rank + LANES, heads * v_dim)
    w_kv = jnp.concatenate([k_w, v_w], axis=-1).astype(BF16)

    return dict(
        w_a=w_a, w_b=w_b, w_uq=w_uq, w_kv=w_kv,
        wuk_t=jnp.transpose(p["w_uk"], (0, 2, 3, 1)).astype(BF16),
        wuv_h=jnp.transpose(p["w_uv"], (0, 2, 1, 3)).astype(BF16),
        w_o_attn=p["w_o_attn"].astype(BF16), w_o_rnn=p["w_o_rnn"].astype(BF16),
        w_out=p["w_out"].astype(BF16), w_up=p["w_up"].astype(BF16), w_down=p["w_down"].astype(BF16),
        w_rg_a=p["w_rg_a"].astype(BF16), w_rg_x=p["w_rg_x"].astype(BF16),
    )


def _rope_tables(past, s, rope, rows):
    half = rope // 2
    pos = past + jnp.arange(s, dtype=jnp.int32)
    inv = ROPE_THETA ** (-jnp.arange(half, dtype=F32) / half)
    ang = pos.astype(F32)[:, None] * inv[None, :]
    zeros = jnp.zeros((s, LANES - rope), F32)
    cos_t = jnp.concatenate([jnp.cos(ang), jnp.cos(ang), zeros], axis=-1)
    sin_t = jnp.concatenate([jnp.sin(ang), jnp.sin(ang), zeros], axis=-1)
    if rows > s:
        cos_t = jnp.tile(cos_t, (rows // s, 1))
        sin_t = jnp.tile(sin_t, (rows // s, 1))
    return cos_t, sin_t


def _run_trunk(x, mod, p, w, dims, *, cache_kv=None, cache_kr=None, h_past=None, conv_past=None):
    b, s, d = x.shape
    m = b * s
    depth = p["w_in"].shape[0]
    q_rank, kv_rank, rope, heads, nope, v_dim, d_rnn = (
        dims[k] for k in ("q_rank", "kv_rank", "rope", "heads", "nope", "v_dim", "d_rnn"))
    alpha = (2 * depth) ** 0.25
    scale = (nope + rope) ** -0.5
    has_past = cache_kv is not None
    past = cache_kv.shape[2] if has_past else 0
    hist = p["conv_w"].shape[1] - 1
    if not has_past:
        h_past = jnp.zeros((1, b, 1, d_rnn), F32)
        conv_past = jnp.zeros((1, b, hist, d_rnn), F32)
    else:
        h_past = h_past.reshape(depth, b, 1, d_rnn)

    tm = _tile(m, 1024)
    assert tm % s == 0 or s % tm == 0
    cos_t, sin_t = _rope_tables(past, s, rope, max(s, tm))
    nt = cos_t.shape[0] // tm
    tab_spec = lambda arr: (arr, (tm, LANES), lambda i, j, k: (i % nt, 0))

    ckv_l, kr_l, h_l, conv_l = [], [], [], []
    x, h = _ln_mod_call(x, p["ln_in_g"], p["ln_in_b"], nmod=mod, mod_layer=0, sc_row=1, sh_row=0)
    tm_a = _tile(m, 512)
    for l in range(depth):
        h2 = h.reshape(m, d)
        q_c, ckv, kr, ckr = _inproj_a_call(
            h2, w["w_a"], l, p["q_norm_g"][l].reshape(1, q_rank), p["kv_norm_g"][l].reshape(1, kv_rank),
            cos_t, sin_t, q_rank=q_rank, kv_rank=kv_rank, rope=rope, tm=tm_a)
        n_b = w["w_b"].shape[-1]
        tn_b = _tile(d_rnn, 512)
        proj_b = _matmul(
            h2, w["w_b"], l, out_dtype=BF16, tn=tn_b, name="inproj_b",
            epilogue=functools.partial(_ep_inproj_b, n_raw=d_rnn // tn_b, n_gelu=d_rnn // tn_b))
        tn_q = _tile(heads * HEAD_W, 512)
        q_aug = _matmul(
            q_c, w["w_uq"], l, out_dtype=BF16, tn=tn_q, name="uq", extra=[tab_spec(cos_t), tab_spec(sin_t)],
            epilogue=functools.partial(_ep_uq, heads=tn_q // HEAD_W, rope=rope, scale=scale))
        q3 = q_aug.reshape(b, s, heads * HEAD_W)
        if has_past:
            o = _attn_dec_call(q3, cache_kv, cache_kr, ckr.reshape(b, s, kv_rank + LANES),
                               w["wuk_t"], w["wuv_h"], l,
                               heads=heads, kv_rank=kv_rank, rope=rope, v_dim=v_dim)
        else:
            kv_aug = _matmul(ckr, w["w_kv"], l, out_dtype=BF16, tn=512, name="kv_expand", epilogue=_ep_store)
            o = _attn_call(q3, kv_aug.reshape(b, s, -1), heads=heads, v_dim=v_dim)
        hb, h_last, conv_n = _rnn_call(
            proj_b.reshape(b, s, n_b), p["conv_w"][l], p["conv_b"][l], w["w_rg_a"], w["w_rg_x"], l,
            p["b_rg_a"][l].reshape(-1), p["b_rg_x"][l].reshape(-1), p["rg_lambda"][l],
            h_past, conv_past, l if has_past else 0, d_rnn=d_rnn, xr_col=0, gy_col=d_rnn)
        merged = _dual_call(o.reshape(m, heads * v_dim), hb.reshape(m, d_rnn), w["w_o_attn"], w["w_o_rnn"], l,
                            proj_b, 2 * d_rnn, 2 * d_rnn + d)
        mix = _matmul(merged, w["w_out"], l, out_dtype=F32, name="w_out", epilogue=_ep_store)
        x, h = _ln_mod_call(x, p["ln_mix_g"][l], p["ln_mix_b"][l], res=mix.reshape(b, s, d), gmod=mod,
                            gate_layer=l, gate_row=2, nmod=mod, mod_layer=l, sc_row=4, sh_row=3, alpha=alpha)
        up = _matmul(h.reshape(m, d), w["w_up"], l, out_dtype=BF16, name="ffn_up", epilogue=_ep_relu2)
        ff = _matmul(up, w["w_down"], l, out_dtype=F32, tn=1024, tk=2048, name="ffn_down", epilogue=_ep_store)
        last = l == depth - 1
        x, h = _ln_mod_call(x, p["ln_ffn_g"][l], p["ln_ffn_b"][l], res=ff.reshape(b, s, d), gmod=mod,
                            gate_layer=l, gate_row=5, nmod=None if last else mod,
                            mod_layer=0 if last else l + 1, sc_row=1, sh_row=0, alpha=alpha)
        ckv_l.append(ckv.reshape(b, s, kv_rank))
        kr_l.append(kr.reshape(b, s, rope))
        h_l.append(h_last.reshape(b, d_rnn))
        conv_l.append(conv_n)
    return x, jnp.stack(ckv_l), jnp.stack(kr_l), jnp.stack(h_l), jnp.stack(conv_l)


def kernel(x_prompt, x_sample, cache_kv_latent, cache_k_rope, state_rglru_h, state_conv, c_prompt, c_sample, ln_in_g, ln_in_b, w_in, q_norm_g, w_uq, kv_norm_g, w_uk, w_uv, w_o_attn, conv_w, conv_b, w_rg_a, b_rg_a, w_rg_x, b_rg_x, rg_lambda, w_o_rnn, w_out, w_mod, b_mod, ln_mix_g, ln_mix_b, w_up, w_down, ln_ffn_g, ln_ffn_b):
    p = dict(ln_in_g=ln_in_g, ln_in_b=ln_in_b, w_in=w_in, q_norm_g=q_norm_g, w_uq=w_uq, kv_norm_g=kv_norm_g,
             w_uk=w_uk, w_uv=w_uv, w_o_attn=w_o_attn, conv_w=conv_w, conv_b=conv_b, w_rg_a=w_rg_a,
             b_rg_a=b_rg_a, w_rg_x=w_rg_x, b_rg_x=b_rg_x, rg_lambda=rg_lambda, w_o_rnn=w_o_rnn, w_out=w_out,
             w_mod=w_mod, b_mod=b_mod, ln_mix_g=ln_mix_g, ln_mix_b=ln_mix_b, w_up=w_up, w_down=w_down,
             ln_ffn_g=ln_ffn_g, ln_ffn_b=ln_ffn_b)
    depth, d_model, _ = w_in.shape
    _, kv_rank, heads, nope = w_uk.shape
    dims = dict(d_model=d_model, q_rank=q_norm_g.shape[-1], kv_rank=kv_rank, heads=heads, nope=nope,
                rope=w_uq.shape[-1] // heads - nope, v_dim=w_uv.shape[-1], d_rnn=conv_w.shape[-1])
    assert dims["nope"] == LANES and dims["rope"] == LANES // 2 and dims["v_dim"] == LANES
    assert dims["d_rnn"] == d_model
    n_mod = w_mod.shape[-1] // d_model

    bp, bs = x_prompt.shape[0], x_sample.shape[0]
    rows = -(-(bp + bs) // SUBLANES) * SUBLANES
    c_all = jnp.concatenate([c_prompt, c_sample, jnp.zeros((rows - bp - bs, d_model), F32)], axis=0)
    mod_all = _mod_call(c_all, w_mod, b_mod).reshape(depth, rows, n_mod, d_model)
    mod_p = mod_all[:, :bp]
    mod_s = mod_all[:, bp:bp + bs]

    w = _prep_weights(p, dims)
    y_p, ckv_p, kr_p, h_p, conv_p = _run_trunk(x_prompt, mod_p, p, w, dims)
    y_s, ckv_s, kr_s, h_s, conv_s = _run_trunk(
        x_sample, mod_s, p, w, dims, cache_kv=cache_kv_latent, cache_kr=cache_k_rope,
        h_past=state_rglru_h, conv_past=state_conv)
    return (y_p, y_s, ckv_p, kr_p, h_p, conv_p, ckv_s, kr_s, h_s, conv_s)
```

```python
import functools
import math

import jax
import jax.numpy as jnp
from jax import lax
from jax.experimental import pallas as pl
from jax.experimental.pallas import tpu as pltpu

F32 = jnp.float32
BF16 = jnp.bfloat16

CHUNK = 64
ROPE_THETA = 10000.0
RG_C = 8.0
LN_EPS = 1e-5
RMS_EPS = 1e-6

LANES = 128
SUBLANES = 8
HEAD_W = 2 * LANES
ONES_ROWS = 2 * SUBLANES
VMEM_LIMIT_BYTES = 56 * 1024 * 1024


def _cparams(*sem):
    return pltpu.CompilerParams(dimension_semantics=sem, vmem_limit_bytes=VMEM_LIMIT_BYTES)


def _tile(n, pref):
    if n <= pref:
        return n
    t = pref
    while n % t:
        t //= 2
    return t


def _mod_kernel(c_ref, w_ref, b_ref, o_ref):
    c = c_ref[...]
    cm = (c * jax.nn.sigmoid(c)).astype(BF16)
    o_ref[0] = jnp.dot(cm, w_ref[0].astype(BF16), preferred_element_type=F32) + b_ref[0]


def _mod_call(c_all, w_mod, b_mod):
    depth, d, n = w_mod.shape
    rows = c_all.shape[0]
    tn = _tile(n, 512)
    return pl.pallas_call(
        _mod_kernel,
        grid=(depth, n // tn),
        in_specs=[
            pl.BlockSpec((rows, d), lambda l, j: (0, 0)),
            pl.BlockSpec((1, d, tn), lambda l, j: (l, 0, j)),
            pl.BlockSpec((1, 1, tn), lambda l, j: (l, 0, j)),
        ],
        out_specs=pl.BlockSpec((1, rows, tn), lambda l, j: (l, 0, j)),
        out_shape=jax.ShapeDtypeStruct((depth, rows, n), F32),
        compiler_params=_cparams("parallel", "arbitrary"),
        name="mod",
    )(c_all, w_mod, b_mod.reshape(depth, 1, n))


def _ln_mod_kernel(*refs, has_res, has_mod, gate_row, sc_row, sh_row, alpha):
    it = iter(refs)
    x_ref = next(it)
    if has_res:
        y_ref = next(it)
        gmod_ref = next(it)
    g_ref = next(it)
    b_ref = next(it)
    if has_mod:
        nmod_ref = next(it)
    xo_ref = next(it)
    if has_mod:
        h_ref = next(it)

    x = x_ref[...]
    if has_res:
        gate = gmod_ref[0, :, gate_row:gate_row + 1, :]
        x = alpha * x + (1.0 + gate) * y_ref[...]
    mu = jnp.mean(x, axis=-1, keepdims=True)
    xc = x - mu
    var = jnp.mean(xc * xc, axis=-1, keepdims=True)
    y = xc * lax.rsqrt(var + LN_EPS) * g_ref[...] + b_ref[...]
    xo_ref[...] = y
    if has_mod:
        sc = nmod_ref[0, :, sc_row:sc_row + 1, :]
        sh = nmod_ref[0, :, sh_row:sh_row + 1, :]
        h_ref[...] = (y * (1.0 + sc) + sh).astype(BF16)


def _ln_mod_call(x, ln_g, ln_b, *, res=None, gmod=None, gate_layer=0, gate_row=0,
                 nmod=None, mod_layer=0, sc_row=0, sh_row=0, alpha=1.0):
    b, s, d = x.shape
    rows = _tile(s, 256)
    nb = _tile(b, max(1, 256 // rows))
    grid = (b // nb, s // rows)
    blk = pl.BlockSpec((nb, rows, d), lambda i, j: (i, j, 0))
    vec = pl.BlockSpec((1, d), lambda i, j: (0, 0))
    has_res = res is not None
    has_mod = nmod is not None
    args, specs = [x], [blk]
    if has_res:
        n_mod = gmod.shape[2]
        args += [res, gmod]
        specs += [blk, pl.BlockSpec((1, nb, n_mod, d), lambda i, j: (gate_layer, i, 0, 0))]
    args += [ln_g.reshape(1, d), ln_b.reshape(1, d)]
    specs += [vec, vec]
    out_shape = [jax.ShapeDtypeStruct((b, s, d), F32)]
    out_specs = [blk]
    if has_mod:
        n_mod = nmod.shape[2]
        args += [nmod]
        specs += [pl.BlockSpec((1, nb, n_mod, d), lambda i, j: (mod_layer, i, 0, 0))]
        out_shape += [jax.ShapeDtypeStruct((b, s, d), BF16)]
        out_specs += [blk]
    kern = functools.partial(_ln_mod_kernel, has_res=has_res, has_mod=has_mod, gate_row=gate_row,
                             sc_row=sc_row, sh_row=sh_row, alpha=alpha)
    outs = pl.pallas_call(
        kern, grid=grid, in_specs=specs, out_specs=out_specs, out_shape=out_shape,
        compiler_params=_cparams("parallel", "parallel"), name="ln_mod",
    )(*args)
    return outs if has_mod else (outs[0], None)


def _mm_kernel(x_ref, w_ref, *rest, nk, n_extra, epilogue):
    extra = rest[:n_extra]
    o_ref = rest[n_extra]
    if nk == 1:
        acc = jnp.dot(x_ref[...], w_ref[...], preferred_element_type=F32)
        epilogue(acc, extra, o_ref)
    else:
        acc_ref = rest[n_extra + 1]
        k = pl.program_id(2)

        @pl.when(k == 0)
        def _():
            acc_ref[...] = jnp.zeros_like(acc_ref)

        acc_ref[...] += jnp.dot(x_ref[...], w_ref[...], preferred_element_type=F32)

        @pl.when(k == nk - 1)
        def _():
            epilogue(acc_ref[...], extra, o_ref)


def _matmul(x, w, layer, *, out_dtype, epilogue, tm=1024, tn=512, tk=None, extra=(), name="matmul"):
    m, kdim = x.shape
    n = w.shape[2]
    tm = _tile(m, tm)
    tn = _tile(n, tn)
    tk = kdim if tk is None else _tile(kdim, tk)
    nk = kdim // tk
    in_specs = [
        pl.BlockSpec((tm, tk), lambda i, j, k: (i, k)),
        pl.BlockSpec((None, tk, tn), lambda i, j, k: (layer, k, j)),
    ]
    args = [x, w]
    for arr, bshape, imap in extra:
        args.append(arr)
        in_specs.append(pl.BlockSpec(bshape, imap))
    scratch = [pltpu.VMEM((tm, tn), F32)] if nk > 1 else []
    kern = functools.partial(_mm_kernel, nk=nk, n_extra=len(extra), epilogue=epilogue)
    return pl.pallas_call(
        kern,
        grid=(m // tm, n // tn, nk),
        in_specs=in_specs,
        out_specs=pl.BlockSpec((tm, tn), lambda i, j, k: (i, j)),
        out_shape=jax.ShapeDtypeStruct((m, n), out_dtype),
        scratch_shapes=scratch,
        compiler_params=_cparams("parallel", "arbitrary", "arbitrary"),
        name=name,
    )(*args)


def _ep_store(acc, extra, o_ref):
    o_ref[...] = acc.astype(o_ref.dtype)


def _ep_relu2(acc, extra, o_ref):
    r = jnp.maximum(acc, 0.0)
    o_ref[...] = (r * r).astype(o_ref.dtype)


def _gelu_tanh(x):
    c = math.sqrt(2.0 / math.pi)
    return x * (0.5 * (1.0 + jnp.tanh(c * (x + 0.044715 * (x * x * x)))))


def _ep_inproj_b(acc, extra, o_ref, *, n_raw, n_gelu):
    j = pl.program_id(1)

    @pl.when(j < n_raw)
    def _():
        o_ref[...] = acc.astype(o_ref.dtype)

    @pl.when((j >= n_raw) & (j < n_raw + n_gelu))
    def _():
        o_ref[...] = _gelu_tanh(acc).astype(o_ref.dtype)

    @pl.when(j >= n_raw + n_gelu)
    def _():
        o_ref[...] = jax.nn.sigmoid(acc).astype(o_ref.dtype)


def _rope_slab(hi, cos, sin, rope):
    return hi * cos + pltpu.roll(hi, rope, axis=1) * sin


def _ep_uq(acc, extra, o_ref, *, heads, rope, scale):
    cos = extra[0][...]
    sin = extra[1][...]
    for hh in range(heads):
        base = hh * HEAD_W
        lo = acc[:, base:base + LANES] * scale
        hi = _rope_slab(acc[:, base + LANES:base + HEAD_W], cos, sin, rope) * scale
        o_ref[:, base:base + LANES] = lo.astype(o_ref.dtype)
        o_ref[:, base + LANES:base + HEAD_W] = hi.astype(o_ref.dtype)


def _inproj_a_kernel(h_ref, w_ref, qg_ref, kvg_ref, cos_ref, sin_ref,
                     qc_ref, ckv_ref, kr_ref, ckr_ref, *, q_rank, kv_rank, rope):
    acc = jnp.dot(h_ref[...], w_ref[...], preferred_element_type=F32)
    q = acc[:, :q_rank]
    qn = q * lax.rsqrt(jnp.mean(q * q, axis=-1, keepdims=True) + RMS_EPS) * qg_ref[...]
    qc_ref[...] = qn.astype(BF16)
    kv = acc[:, q_rank:q_rank + kv_rank]
    kvn = kv * lax.rsqrt(jnp.mean(kv * kv, axis=-1, keepdims=True) + RMS_EPS) * kvg_ref[...]
    ckv_ref[...] = kvn
    kr = _rope_slab(acc[:, q_rank + kv_rank:], cos_ref[...], sin_ref[...], rope)
    kr_ref[...] = kr[:, :rope]
    ckr_ref[:, :kv_rank] = kvn.astype(BF16)
    ckr_ref[:, kv_rank:] = kr.astype(BF16)


def _inproj_a_call(h, w_a, layer, qg, kvg, cos_t, sin_t, *, q_rank, kv_rank, rope, tm):
    m, d = h.shape
    na = w_a.shape[2]
    nt = cos_t.shape[0] // tm
    row = lambda i: (i, 0)
    fixed = lambda i: (0, 0)
    tab = lambda i: (i % nt, 0)
    kern = functools.partial(_inproj_a_kernel, q_rank=q_rank, kv_rank=kv_rank, rope=rope)
    return pl.pallas_call(
        kern,
        grid=(m // tm,),
        in_specs=[
            pl.BlockSpec((tm, d), row),
            pl.BlockSpec((None, d, na), lambda i: (layer, 0, 0), pipeline_mode=pl.Buffered(1)),
            pl.BlockSpec((1, q_rank), fixed),
            pl.BlockSpec((1, kv_rank), fixed),
            pl.BlockSpec((tm, LANES), tab),
            pl.BlockSpec((tm, LANES), tab),
        ],
        out_specs=[
            pl.BlockSpec((tm, q_rank), row),
            pl.BlockSpec((tm, kv_rank), row),
            pl.BlockSpec((tm, rope), row),
            pl.BlockSpec((tm, kv_rank + LANES), row),
        ],
        out_shape=[
            jax.ShapeDtypeStruct((m, q_rank), BF16),
            jax.ShapeDtypeStruct((m, kv_rank), F32),
            jax.ShapeDtypeStruct((m, rope), F32),
            jax.ShapeDtypeStruct((m, kv_rank + LANES), BF16),
        ],
        compiler_params=_cparams("parallel"),
        name="inproj_a",
    )(h, w_a, qg, kvg, cos_t, sin_t)


def _dual_kernel(a_ref, b_ref, wa_ref, wb_ref, ga_ref, gb_ref, o_ref):
    ya = jnp.dot(a_ref[...], wa_ref[...], preferred_element_type=F32)
    yb = jnp.dot(b_ref[...], wb_ref[...], preferred_element_type=F32)
    o_ref[...] = (ga_ref[...].astype(F32) * ya + gb_ref[...].astype(F32) * yb).astype(o_ref.dtype)


def _dual_call(a, b, wa, wb, layer, gates_src, ga_col, gb_col, *, tm=512, tn=512):
    m, ka = a.shape
    kb = b.shape[1]
    n = wa.shape[2]
    tm = _tile(m, tm)
    tn = _tile(n, tn)
    ga_blk, gb_blk = ga_col // tn, gb_col // tn
    return pl.pallas_call(
        _dual_kernel,
        grid=(m // tm, n // tn),
        in_specs=[
            pl.BlockSpec((tm, ka), lambda i, j: (i, 0)),
            pl.BlockSpec((tm, kb), lambda i, j: (i, 0)),
            pl.BlockSpec((None, ka, tn), lambda i, j: (layer, 0, j)),
            pl.BlockSpec((None, kb, tn), lambda i, j: (layer, 0, j)),
            pl.BlockSpec((tm, tn), lambda i, j: (i, ga_blk + j)),
            pl.BlockSpec((tm, tn), lambda i, j: (i, gb_blk + j)),
        ],
        out_specs=pl.BlockSpec((tm, tn), lambda i, j: (i, j)),
        out_shape=jax.ShapeDtypeStruct((m, n), BF16),
        compiler_params=_cparams("parallel", "arbitrary"),
        name="branch_merge",
    )(a, b, wa, wb, gates_src, gates_src)


def _rnn_kernel(xr_ref, gy_ref, cw_ref, cb_ref, wa_ref, wx_ref, ba_ref, bx_ref, lam_ref, h0_ref, cp_ref,
                hb_ref, hl_ref, cn_ref, xbuf, hcar, *, ts, conv_w):
    t = pl.program_id(2)
    pad = SUBLANES
    hist = conv_w - 1

    @pl.when(t == 0)
    def _():
        xbuf[0:pad, :] = jnp.zeros((pad, xbuf.shape[1]), F32)
        xbuf[pad - hist:pad, :] = cp_ref[0, 0]
        hcar[...] = h0_ref[0, 0]

    x = xr_ref[0].astype(F32)
    xbuf[pad:pad + ts, :] = x
    xc = cb_ref[...] + cw_ref[hist:hist + 1, :] * x
    for k in range(hist):
        d = hist - k
        xc = xc + cw_ref[k:k + 1, :] * xbuf[pad - d:pad - d + ts, :]
    cn_ref[0] = xbuf[pad + ts - hist:pad + ts, :]
    xbuf[0:pad, :] = xbuf[ts:ts + pad, :]

    xb = xc.astype(BF16)
    r = jax.nn.sigmoid(jnp.dot(xb, wa_ref[0], preferred_element_type=F32) + ba_ref[...])
    ig = jax.nn.sigmoid(jnp.dot(xb, wx_ref[0], preferred_element_type=F32) + bx_ref[...])
    nl = -lam_ref[...]
    softplus = jnp.maximum(nl, 0.0) + jnp.log1p(jnp.exp(-jnp.abs(nl)))
    log_a = (-RG_C) * r * softplus
    a = jnp.exp(log_a)
    gain = jnp.sqrt(-jnp.tanh(log_a) * (a * a + 1.0))
    bv = gain * (ig * xc)

    ng = ts // SUBLANES
    a3 = a.reshape(ng, SUBLANES, a.shape[1])
    b3 = bv.reshape(ng, SUBLANES, a.shape[1])
    sub = lax.broadcasted_iota(jnp.int32, a3.shape, 1)
    d = 1
    while d < SUBLANES:
        keep = sub >= d
        b3 = jnp.where(keep, a3 * pltpu.roll(b3, d, axis=1) + b3, b3)
        a3 = jnp.where(keep, a3 * pltpu.roll(a3, d, axis=1), a3)
        d *= 2
    carry = hcar[...]
    gy = gy_ref[0].astype(F32).reshape(a3.shape)
    for g in range(0, ng, 2):
        outs = []
        for gg in (g, g + 1):
            hg = a3[gg] * carry + b3[gg]
            carry = hg[SUBLANES - 1:SUBLANES, :]
            outs.append(hg * gy[gg])
        hb_ref[0, g * SUBLANES:(g + 2) * SUBLANES, :] = jnp.concatenate(outs, axis=0).astype(BF16)
    hcar[...] = carry
    hl_ref[0] = carry


def _rnn_call(proj_b, conv_w_l, conv_b_l, wa, wx, wlayer, ba, bx, lam, h0, cpast, layer, *, d_rnn, xr_col, gy_col):
    b, s, _ = proj_b.shape
    _, nblk, bw, _ = wa.shape
    conv_w = conv_w_l.shape[0]
    hist = conv_w - 1
    assert s >= hist and s % SUBLANES == 0
    ts = _tile(s, 512)
    xr_blk, gy_blk = xr_col // bw, gy_col // bw
    vec = lambda bi, n, t: (0, n)
    kern = functools.partial(_rnn_kernel, ts=ts, conv_w=conv_w)
    return pl.pallas_call(
        kern,
        grid=(b, nblk, s // ts),
        in_specs=[
            pl.BlockSpec((1, ts, bw), lambda bi, n, t: (bi, t, xr_blk + n)),
            pl.BlockSpec((1, ts, bw), lambda bi, n, t: (bi, t, gy_blk + n)),
            pl.BlockSpec((conv_w, bw), vec),
            pl.BlockSpec((1, bw), vec),
            pl.BlockSpec((None, 1, bw, bw), lambda bi, n, t: (wlayer, n, 0, 0)),
            pl.BlockSpec((None, 1, bw, bw), lambda bi, n, t: (wlayer, n, 0, 0)),
            pl.BlockSpec((1, bw), vec),
            pl.BlockSpec((1, bw), vec),
            pl.BlockSpec((1, bw), vec),
            pl.BlockSpec((1, 1, 1, bw), lambda bi, n, t: (layer, bi, 0, n)),
            pl.BlockSpec((1, 1, hist, bw), lambda bi, n, t: (layer, bi, 0, n)),
        ],
        out_specs=[
            pl.BlockSpec((1, ts, bw), lambda bi, n, t: (bi, t, n)),
            pl.BlockSpec((1, 1, bw), lambda bi, n, t: (bi, 0, n)),
            pl.BlockSpec((1, hist, bw), lambda bi, n, t: (bi, 0, n)),
        ],
        out_shape=[
            jax.ShapeDtypeStruct((b, s, d_rnn), BF16),
            jax.ShapeDtypeStruct((b, 1, d_rnn), F32),
            jax.ShapeDtypeStruct((b, hist, d_rnn), F32),
        ],
        scratch_shapes=[pltpu.VMEM((SUBLANES + ts, bw), F32), pltpu.VMEM((1, bw), F32)],
        compiler_params=_cparams("parallel", "parallel", "arbitrary"),
        name="rglru",
    )(proj_b, proj_b, conv_w_l, conv_b_l.reshape(1, d_rnn), wa, wx, ba.reshape(1, d_rnn),
      bx.reshape(1, d_rnn), lam.reshape(1, d_rnn), h0, cpast)


def _attn_kernel(q_ref, k_ref, vt_ref, o_ref, s0, s1, p0, p1, a0, a1, m_sc, acc_sc, *, tq, v_dim):
    i = pl.program_id(2)
    n = i + 1
    s_buf, p_buf, a_buf = (s0, s1), (p0, p1), (a0, a1)
    q = q_ref[0]
    m_sc[...] = jnp.full(m_sc.shape, -jnp.inf, F32)
    acc_sc[...] = jnp.zeros(acc_sc.shape, F32)

    def kv_off(t):
        if isinstance(t, int):
            idx = i if t == 0 else t - 1
        else:
            idx = jnp.where(t == 0, i, t - 1)
        return pl.multiple_of(idx * tq, tq)

    def stage_a(t, slot, masked=False):
        k = k_ref[0, pl.ds(kv_off(t), tq), :]
        st = lax.dot_general(k, q, (((1,), (1,)), ((), ())), preferred_element_type=F32)
        if masked:
            kc = lax.broadcasted_iota(jnp.int32, st.shape, 0) // CHUNK
            qc = lax.broadcasted_iota(jnp.int32, st.shape, 1) // CHUNK
            st = jnp.where(kc <= qc, st, jnp.finfo(F32).min)
        s_buf[slot][...] = st

    def stage_b(slot):
        m_prev = m_sc[...]
        m_new = jnp.maximum(m_prev, jnp.max(s_buf[slot][...], axis=0, keepdims=True))
        m_sc[...] = m_new
        a_buf[slot][...] = jnp.exp2(m_prev - m_new)
        p_buf[slot][...] = jnp.exp2(s_buf[slot][...] - m_new).astype(BF16)

    def stage_c(t, slot):
        vt = vt_ref[0, :, pl.ds(kv_off(t), tq)]
        acc_sc[...] = a_buf[slot][...] * acc_sc[...] + jnp.dot(vt, p_buf[slot][...], preferred_element_type=F32)

    def half_step(t, par):
        stage_a(t, par)
        stage_b(1 - par)
        stage_c(t - 2, par)

    stage_a(0, 0, masked=True)

    @pl.when(n == 1)
    def _():
        stage_b(0)
        stage_c(0, 0)

    @pl.when(n >= 2)
    def _():
        stage_a(1, 1)
        stage_b(0)

        def body(u, carry):
            t = 2 * u
            half_step(t, 0)
            half_step(t + 1, 1)
            return carry

        lax.fori_loop(1, n // 2, body, 0)

        @pl.when(n % 2 == 1)
        def _():
            half_step(n - 1, 0)
            stage_b(0)
            stage_c(n - 2, 1)
            stage_c(n - 1, 0)

        @pl.when(n % 2 == 0)
        def _():
            stage_b(1)
            stage_c(n - 2, 0)
            stage_c(n - 1, 1)

    o_ref[0] = (acc_sc[0:v_dim, :] / acc_sc[v_dim:v_dim + 1, :]).T.astype(o_ref.dtype)


def _attn_call(q_aug, k_aug, vt, *, heads, v_dim):
    b, s, _ = q_aug.shape
    tq = _tile(s, 512)
    assert tq % CHUNK == 0
    vrows = v_dim + ONES_ROWS
    kern = functools.partial(_attn_kernel, tq=tq, v_dim=v_dim)
    score = pltpu.VMEM((tq, tq), F32)
    prob = pltpu.VMEM((tq, tq), BF16)
    stat = pltpu.VMEM((1, tq), F32)
    return pl.pallas_call(
        kern,
        grid=(b, heads, s // tq),
        in_specs=[
            pl.BlockSpec((1, tq, HEAD_W), lambda bi, h, i: (bi, i, h)),
            pl.BlockSpec((1, s, HEAD_W), lambda bi, h, i: (bi, 0, h)),
            pl.BlockSpec((1, vrows, s), lambda bi, h, i: (bi, h, 0)),
        ],
        out_specs=pl.BlockSpec((1, tq, v_dim), lambda bi, h, i: (bi, i, h)),
        out_shape=jax.ShapeDtypeStruct((b, s, heads * v_dim), BF16),
        scratch_shapes=[score, score, prob, prob, stat, stat, stat, pltpu.VMEM((vrows, tq), F32)],
        compiler_params=_cparams("parallel", "parallel", "arbitrary"),
        name="attn_prompt",
    )(q_aug, k_aug, vt)


def _vt_expand_kernel(w_ref, x_ref, o_ref, *, hg, v_dim):
    acc = lax.dot_general(w_ref[...], x_ref[0], (((1,), (1,)), ((), ())), preferred_element_type=F32)
    for hh in range(hg):
        base = hh * (v_dim + ONES_ROWS)
        o_ref[0, base:base + v_dim, :] = acc[hh * v_dim:(hh + 1) * v_dim, :].astype(o_ref.dtype)
        o_ref[0, base + v_dim:base + v_dim + ONES_ROWS, :] = jnp.ones((ONES_ROWS, acc.shape[1]), o_ref.dtype)


def _vt_expand_call(ckr, w_vt, layer, *, v_dim):
    b, s, f = ckr.shape
    n = w_vt.shape[1]
    tr = _tile(n, 512)
    ts = _tile(s, 1024)
    hg = tr // v_dim
    rows = hg * (v_dim + ONES_ROWS)
    return pl.pallas_call(
        functools.partial(_vt_expand_kernel, hg=hg, v_dim=v_dim),
        grid=(b, s // ts, n // tr),
        in_specs=[
            pl.BlockSpec((None, tr, f), lambda bi, t, r: (layer, r, 0)),
            pl.BlockSpec((1, ts, f), lambda bi, t, r: (bi, t, 0)),
        ],
        out_specs=pl.BlockSpec((1, rows, ts), lambda bi, t, r: (bi, r, t)),
        out_shape=jax.ShapeDtypeStruct((b, n // tr * rows, s), BF16),
        compiler_params=_cparams("parallel", "parallel", "arbitrary"),
        name="vt_expand",
    )(w_vt, ckr)


def _attn_dec_kernel(q_ref, cp_ref, kp_ref, cn_ref, wuk_ref, wuv_ref, o_ref, cpb, kpb, *, hg, kv_rank, rope, s):
    g = pl.program_id(1)

    @pl.when(g == 0)
    def _():
        cpb[...] = cp_ref[0, 0].astype(BF16)
        kpb[...] = kp_ref[0, 0].astype(BF16)

    nt = (((1,), (1,)), ((), ()))
    ql, qr = [], []
    for hh in range(hg):
        base = hh * HEAD_W
        qn = q_ref[0, :, base:base + LANES]
        ql.append(jnp.dot(qn, wuk_ref[hh], preferred_element_type=F32).astype(BF16))
        qr.append(q_ref[0, :, base + LANES:base + LANES + rope])
    ql = jnp.concatenate(ql, axis=0)
    qr = jnp.concatenate(qr, axis=0)
    c_new = cn_ref[0, :, :kv_rank]
    k_new = cn_ref[0, :, kv_rank:kv_rank + rope]
    c_past = cpb[...]
    s_past = (lax.dot_general(ql, c_past, nt, preferred_element_type=F32)
              + lax.dot_general(qr, kpb[...], nt, preferred_element_type=F32))
    s_new = (lax.dot_general(ql, c_new, nt, preferred_element_type=F32)
             + lax.dot_general(qr, k_new, nt, preferred_element_type=F32))
    m = jnp.maximum(jnp.max(s_past, axis=-1, keepdims=True), jnp.max(s_new, axis=-1, keepdims=True))
    p_past = jnp.exp2(s_past - m)
    p_new = jnp.exp2(s_new - m)
    den = jnp.sum(p_past, axis=-1, keepdims=True) + jnp.sum(p_new, axis=-1, keepdims=True)
    o_lat = (jnp.dot(p_past.astype(BF16), c_past, preferred_element_type=F32)
             + jnp.dot(p_new.astype(BF16), c_new, preferred_element_type=F32)) / den
    o_lat = o_lat.astype(BF16)
    for hh in range(hg):
        oh = jnp.dot(o_lat[hh * s:(hh + 1) * s, :], wuv_ref[hh], preferred_element_type=F32)
        o_ref[0, :, hh * LANES:(hh + 1) * LANES] = oh.astype(o_ref.dtype)


def _attn_dec_call(q_aug, cache_kv, cache_kr, ckr, wuk_t, wuv_h, layer, *, heads, kv_rank, rope, v_dim):
    b, s, _ = q_aug.shape
    past = cache_kv.shape[2]
    assert past % CHUNK == 0 and s <= CHUNK and v_dim == LANES
    hg = _tile(heads, 4)
    kern = functools.partial(_attn_dec_kernel, hg=hg, kv_rank=kv_rank, rope=rope, s=s)
    return pl.pallas_call(
        kern,
        grid=(b, heads // hg),
        in_specs=[
            pl.BlockSpec((1, s, hg * HEAD_W), lambda bi, g: (bi, 0, g)),
            pl.BlockSpec((1, 1, past, kv_rank), lambda bi, g: (layer, bi, 0, 0)),
            pl.BlockSpec((1, 1, past, rope), lambda bi, g: (layer, bi, 0, 0)),
            pl.BlockSpec((1, s, kv_rank + LANES), lambda bi, g: (bi, 0, 0)),
            pl.BlockSpec((None, hg, LANES, kv_rank), lambda bi, g: (layer, g, 0, 0)),
            pl.BlockSpec((None, hg, kv_rank, v_dim), lambda bi, g: (layer, g, 0, 0)),
        ],
        out_specs=pl.BlockSpec((1, s, hg * v_dim), lambda bi, g: (bi, 0, g)),
        out_shape=jax.ShapeDtypeStruct((b, s, heads * v_dim), BF16),
        scratch_shapes=[pltpu.VMEM((past, kv_rank), BF16), pltpu.VMEM((past, rope), BF16)],
        compiler_params=_cparams("parallel", "arbitrary"),
        name="attn_sample",
    )(q_aug, cache_kv, cache_kr, ckr, wuk_t, wuv_h)


def _rotate_half_cols(w, half):
    return jnp.concatenate([-w[..., half:], w[..., :half]], axis=-1)


def _prep_weights(p, dims):
    q_rank, kv_rank, rope, heads, nope, v_dim, d_rnn, d_model = (
        dims[k] for k in ("q_rank", "kv_rank", "rope", "heads", "nope", "v_dim", "d_rnn", "d_model"))
    half = rope // 2
    depth = p["w_in"].shape[0]
    off_kr = q_rank + kv_rank
    off_rx = off_kr + rope
    w_in = p["w_in"]
    w_kr = w_in[:, :, off_kr:off_rx]
    w_a = jnp.concatenate([w_in[:, :, :off_kr], w_kr, _rotate_half_cols(w_kr, half)], axis=-1).astype(BF16)
    w_b = w_in[:, :, off_rx:].astype(BF16)

    wq = p["w_uq"].reshape(depth, q_rank, heads, nope + rope)
    wq_r = wq[..., nope:]
    w_uq = jnp.concatenate([wq[..., :nope], wq_r, _rotate_half_cols(wq_r, half)], axis=-1)
    w_uq = w_uq.reshape(depth, q_rank, heads * HEAD_W).astype(BF16)

    zk = jnp.zeros((depth, kv_rank, heads, HEAD_W - nope), F32)
    k_top = jnp.concatenate([p["w_uk"], zk], axis=-1)
    eye = jnp.zeros((LANES, HEAD_W), F32).at[jnp.arange(rope), nope + jnp.arange(rope)].set(1.0)
    k_bot = jnp.broadcast_to(eye[None, :, None, :], (depth, LANES, heads, HEAD_W))
    k_w = jnp.concatenate([k_top, k_bot], axis=1).reshape(depth, kv_rank + LANES, heads * HEAD_W)
    v_w = jnp.concatenate([p["w_uv"], jnp.zeros((depth, LANES, heads, v_dim), F32)], axis=1)
    w_vt = jnp.transpose(v_w, (0, 2, 3, 1)).reshape(depth, heads * v_dim, kv_rank + LANES).astype(BF16)

    return dict(
        w_a=w_a, w_b=w_b, w_uq=w_uq, w_k=k_w.astype(BF16), w_vt=w_vt,
        wuk_t=jnp.transpose(p["w_uk"], (0, 2, 3, 1)).astype(BF16),
        wuv_h=jnp.transpose(p["w_uv"], (0, 2, 1, 3)).astype(BF16),
        w_o_attn=p["w_o_attn"].astype(BF16), w_o_rnn=p["w_o_rnn"].astype(BF16),
        w_out=p["w_out"].astype(BF16), w_up=p["w_up"].astype(BF16), w_down=p["w_down"].astype(BF16),
        w_rg_a=p["w_rg_a"].astype(BF16), w_rg_x=p["w_rg_x"].astype(BF16),
    )


def _rope_tables(past, s, rope, rows):
    half = rope // 2
    pos = past + jnp.arange(s, dtype=jnp.int32)
    inv = ROPE_THETA ** (-jnp.arange(half, dtype=F32) / half)
    ang = pos.astype(F32)[:, None] * inv[None, :]
    zeros = jnp.zeros((s, LANES - rope), F32)
    cos_t = jnp.concatenate([jnp.cos(ang), jnp.cos(ang), zeros], axis=-1)
    sin_t = jnp.concatenate([jnp.sin(ang), jnp.sin(ang), zeros], axis=-1)
    if rows > s:
        cos_t = jnp.tile(cos_t, (rows // s, 1))
        sin_t = jnp.tile(sin_t, (rows // s, 1))
    return cos_t, sin_t


def _run_trunk(x, mod, p, w, dims, *, cache_kv=None, cache_kr=None, h_past=None, conv_past=None):
    b, s, d = x.shape
    m = b * s
    depth = p["w_in"].shape[0]
    q_rank, kv_rank, rope, heads, nope, v_dim, d_rnn = (
        dims[k] for k in ("q_rank", "kv_rank", "rope", "heads", "nope", "v_dim", "d_rnn"))
    alpha = (2 * depth) ** 0.25
    scale = (nope + rope) ** -0.5 * math.log2(math.e)
    has_past = cache_kv is not None
    past = cache_kv.shape[2] if has_past else 0
    hist = p["conv_w"].shape[1] - 1
    if not has_past:
        h_past = jnp.zeros((1, b, 1, d_rnn), F32)
        conv_past = jnp.zeros((1, b, hist, d_rnn), F32)
    else:
        h_past = h_past.reshape(depth, b, 1, d_rnn)

    tm = _tile(m, 1024)
    assert tm % s == 0 or s % tm == 0
    cos_t, sin_t = _rope_tables(past, s, rope, max(s, tm))
    nt = cos_t.shape[0] // tm
    tab_spec = lambda arr: (arr, (tm, LANES), lambda i, j, k: (i % nt, 0))

    ckv_l, kr_l, h_l, conv_l = [], [], [], []
    x, h = _ln_mod_call(x, p["ln_in_g"], p["ln_in_b"], nmod=mod, mod_layer=0, sc_row=1, sh_row=0)
    tm_a = _tile(m, 512)
    for l in range(depth):
        h2 = h.reshape(m, d)
        q_c, ckv, kr, ckr = _inproj_a_call(
            h2, w["w_a"], l, p["q_norm_g"][l].reshape(1, q_rank), p["kv_norm_g"][l].reshape(1, kv_rank),
            cos_t, sin_t, q_rank=q_rank, kv_rank=kv_rank, rope=rope, tm=tm_a)
        n_b = w["w_b"].shape[-1]
        tn_b = _tile(d_rnn, 512)
        proj_b = _matmul(
            h2, w["w_b"], l, out_dtype=BF16, tn=tn_b, name="inproj_b",
            epilogue=functools.partial(_ep_inproj_b, n_raw=d_rnn // tn_b, n_gelu=d_rnn // tn_b))
        tn_q = _tile(heads * HEAD_W, 512)
        q_aug = _matmul(
            q_c, w["w_uq"], l, out_dtype=BF16, tn=tn_q, name="uq", extra=[tab_spec(cos_t), tab_spec(sin_t)],
            epilogue=functools.partial(_ep_uq, heads=tn_q // HEAD_W, rope=rope, scale=scale))
        q3 = q_aug.reshape(b, s, heads * HEAD_W)
        if has_past:
            o = _attn_dec_call(q3, cache_kv, cache_kr, ckr.reshape(b, s, kv_rank + LANES),
                               w["wuk_t"], w["wuv_h"], l,
                               heads=heads, kv_rank=kv_rank, rope=rope, v_dim=v_dim)
        else:
            k_aug = _matmul(ckr, w["w_k"], l, out_dtype=BF16, tn=512, name="k_expand", epilogue=_ep_store)
            vt = _vt_expand_call(ckr.reshape(b, s, kv_rank + LANES), w["w_vt"], l, v_dim=v_dim)
            o = _attn_call(q3, k_aug.reshape(b, s, heads * HEAD_W), vt, heads=heads, v_dim=v_dim)
        hb, h_last, conv_n = _rnn_call(
            proj_b.reshape(b, s, n_b), p["conv_w"][l], p["conv_b"][l], w["w_rg_a"], w["w_rg_x"], l,
            p["b_rg_a"][l].reshape(-1), p["b_rg_x"][l].reshape(-1), p["rg_lambda"][l],
            h_past, conv_past, l if has_past else 0, d_rnn=d_rnn, xr_col=0, gy_col=d_rnn)
        merged = _dual_call(o.reshape(m, heads * v_dim), hb.reshape(m, d_rnn), w["w_o_attn"], w["w_o_rnn"], l,
                            proj_b, 2 * d_rnn, 2 * d_rnn + d)
        mix = _matmul(merged, w["w_out"], l, out_dtype=F32, name="w_out", epilogue=_ep_store)
        x, h = _ln_mod_call(x, p["ln_mix_g"][l], p["ln_mix_b"][l], res=mix.reshape(b, s, d), gmod=mod,
                            gate_layer=l, gate_row=2, nmod=mod, mod_layer=l, sc_row=4, sh_row=3, alpha=alpha)
        up = _matmul(h.reshape(m, d), w["w_up"], l, out_dtype=BF16, name="ffn_up", epilogue=_ep_relu2)
        ff = _matmul(up, w["w_down"], l, out_dtype=F32, tn=1024, tk=2048, name="ffn_down", epilogue=_ep_store)
        last = l == depth - 1
        x, h = _ln_mod_call(x, p["ln_ffn_g"][l], p["ln_ffn_b"][l], res=ff.reshape(b, s, d), gmod=mod,
                            gate_layer=l, gate_row=5, nmod=None if last else mod,
                            mod_layer=0 if last else l + 1, sc_row=1, sh_row=0, alpha=alpha)
        ckv_l.append(ckv.reshape(b, s, kv_rank))
        kr_l.append(kr.reshape(b, s, rope))
        h_l.append(h_last.reshape(b, d_rnn))
        conv_l.append(conv_n)
    return x, jnp.stack(ckv_l), jnp.stack(kr_l), jnp.stack(h_l), jnp.stack(conv_l)


def kernel(x_prompt, x_sample, cache_kv_latent, cache_k_rope, state_rglru_h, state_conv, c_prompt, c_sample, ln_in_g, ln_in_b, w_in, q_norm_g, w_uq, kv_norm_g, w_uk, w_uv, w_o_attn, conv_w, conv_b, w_rg_a, b_rg_a, w_rg_x, b_rg_x, rg_lambda, w_o_rnn, w_out, w_mod, b_mod, ln_mix_g, ln_mix_b, w_up, w_down, ln_ffn_g, ln_ffn_b):
    p = dict(ln_in_g=ln_in_g, ln_in_b=ln_in_b, w_in=w_in, q_norm_g=q_norm_g, w_uq=w_uq, kv_norm_g=kv_norm_g,
             w_uk=w_uk, w_uv=w_uv, w_o_attn=w_o_attn, conv_w=conv_w, conv_b=conv_b, w_rg_a=w_rg_a,
             b_rg_a=b_rg_a, w_rg_x=w_rg_x, b_rg_x=b_rg_x, rg_lambda=rg_lambda, w_o_rnn=w_o_rnn, w_out=w_out,
             w_mod=w_mod, b_mod=b_mod, ln_mix_g=ln_mix_g, ln_mix_b=ln_mix_b, w_up=w_up, w_down=w_down,
             ln_ffn_g=ln_ffn_g, ln_ffn_b=ln_ffn_b)
    depth, d_model, _ = w_in.shape
    _, kv_rank, heads, nope = w_uk.shape
    dims = dict(d_model=d_model, q_rank=q_norm_g.shape[-1], kv_rank=kv_rank, heads=heads, nope=nope,
                rope=w_uq.shape[-1] // heads - nope, v_dim=w_uv.shape[-1], d_rnn=conv_w.shape[-1])
    assert dims["nope"] == LANES and dims["rope"] == LANES // 2 and dims["v_dim"] == LANES
    assert dims["d_rnn"] == d_model
    n_mod = w_mod.shape[-1] // d_model

    bp, bs = x_prompt.shape[0], x_sample.shape[0]
    rows = -(-(bp + bs) // SUBLANES) * SUBLANES
    c_all = jnp.concatenate([c_prompt, c_sample, jnp.zeros((rows - bp - bs, d_model), F32)], axis=0)
    mod_all = _mod_call(c_all, w_mod, b_mod).reshape(depth, rows, n_mod, d_model)
    mod_p = mod_all[:, :bp]
    mod_s = mod_all[:, bp:bp + bs]

    w = _prep_weights(p, dims)
    y_p, ckv_p, kr_p, h_p, conv_p = _run_trunk(x_prompt, mod_p, p, w, dims)
    y_s, ckv_s, kr_s, h_s, conv_s = _run_trunk(
        x_sample, mod_s, p, w, dims, cache_kv=cache_kv_latent, cache_kr=cache_k_rope,
        h_past=state_rglru_h, conv_past=state_conv)
    return (y_p, y_s, ckv_p, kr_p, h_p, conv_p, ckv_s, kr_s, h_s, conv_s)
```

```python
import functools
import math

import jax
import jax.numpy as jnp
from jax import lax
from jax.experimental import pallas as pl
from jax.experimental.pallas import tpu as pltpu

F32 = jnp.float32
BF16 = jnp.bfloat16

CHUNK = 64
ROPE_THETA = 10000.0
RG_C = 8.0
LN_EPS = 1e-5
RMS_EPS = 1e-6

LANES = 128
SUBLANES = 8
HEAD_W = 2 * LANES
ONES_ROWS = 2 * SUBLANES
VMEM_LIMIT_BYTES = 56 * 1024 * 1024


def _cparams(*sem):
    return pltpu.CompilerParams(dimension_semantics=sem, vmem_limit_bytes=VMEM_LIMIT_BYTES)


def _tile(n, pref):
    if n <= pref:
        return n
    t = pref
    while n % t:
        t //= 2
    return t


def _mod_kernel(c_ref, w_ref, b_ref, o_ref):
    c = c_ref[...]
    cm = (c * jax.nn.sigmoid(c)).astype(BF16)
    o_ref[0] = jnp.dot(cm, w_ref[0].astype(BF16), preferred_element_type=F32) + b_ref[0]


def _mod_call(c_all, w_mod, b_mod):
    depth, d, n = w_mod.shape
    rows = c_all.shape[0]
    tn = _tile(n, 512)
    return pl.pallas_call(
        _mod_kernel,
        grid=(depth, n // tn),
        in_specs=[
            pl.BlockSpec((rows, d), lambda l, j: (0, 0)),
            pl.BlockSpec((1, d, tn), lambda l, j: (l, 0, j)),
            pl.BlockSpec((1, 1, tn), lambda l, j: (l, 0, j)),
        ],
        out_specs=pl.BlockSpec((1, rows, tn), lambda l, j: (l, 0, j)),
        out_shape=jax.ShapeDtypeStruct((depth, rows, n), F32),
        compiler_params=_cparams("parallel", "arbitrary"),
        name="mod",
    )(c_all, w_mod, b_mod.reshape(depth, 1, n))


def _ln_mod_kernel(*refs, has_res, has_mod, gate_row, sc_row, sh_row, alpha):
    it = iter(refs)
    x_ref = next(it)
    if has_res:
        y_ref = next(it)
        gmod_ref = next(it)
    g_ref = next(it)
    b_ref = next(it)
    if has_mod:
        nmod_ref = next(it)
    xo_ref = next(it)
    if has_mod:
        h_ref = next(it)

    x = x_ref[...]
    if has_res:
        gate = gmod_ref[0, :, gate_row:gate_row + 1, :]
        x = alpha * x + (1.0 + gate) * y_ref[...]
    mu = jnp.mean(x, axis=-1, keepdims=True)
    xc = x - mu
    var = jnp.mean(xc * xc, axis=-1, keepdims=True)
    y = xc * lax.rsqrt(var + LN_EPS) * g_ref[...] + b_ref[...]
    xo_ref[...] = y
    if has_mod:
        sc = nmod_ref[0, :, sc_row:sc_row + 1, :]
        sh = nmod_ref[0, :, sh_row:sh_row + 1, :]
        h_ref[...] = (y * (1.0 + sc) + sh).astype(BF16)


def _ln_mod_call(x, ln_g, ln_b, *, res=None, gmod=None, gate_layer=0, gate_row=0,
                 nmod=None, mod_layer=0, sc_row=0, sh_row=0, alpha=1.0):
    b, s, d = x.shape
    rows = _tile(s, 256)
    nb = _tile(b, max(1, 256 // rows))
    grid = (b // nb, s // rows)
    blk = pl.BlockSpec((nb, rows, d), lambda i, j: (i, j, 0))
    vec = pl.BlockSpec((1, d), lambda i, j: (0, 0))
    has_res = res is not None
    has_mod = nmod is not None
    args, specs = [x], [blk]
    if has_res:
        n_mod = gmod.shape[2]
        args += [res, gmod]
        specs += [blk, pl.BlockSpec((1, nb, n_mod, d), lambda i, j: (gate_layer, i, 0, 0))]
    args += [ln_g.reshape(1, d), ln_b.reshape(1, d)]
    specs += [vec, vec]
    out_shape = [jax.ShapeDtypeStruct((b, s, d), F32)]
    out_specs = [blk]
    if has_mod:
        n_mod = nmod.shape[2]
        args += [nmod]
        specs += [pl.BlockSpec((1, nb, n_mod, d), lambda i, j: (mod_layer, i, 0, 0))]
        out_shape += [jax.ShapeDtypeStruct((b, s, d), BF16)]
        out_specs += [blk]
    kern = functools.partial(_ln_mod_kernel, has_res=has_res, has_mod=has_mod, gate_row=gate_row,
                             sc_row=sc_row, sh_row=sh_row, alpha=alpha)
    outs = pl.pallas_call(
        kern, grid=grid, in_specs=specs, out_specs=out_specs, out_shape=out_shape,
        compiler_params=_cparams("parallel", "parallel"), name="ln_mod",
    )(*args)
    return outs if has_mod else (outs[0], None)


_NT_DIMS = (((1,), (1,)), ((), ()))


def _mm_kernel(x_ref, w_ref, *rest, nk, n_extra, epilogue, n_sub):
    extra = rest[:n_extra]
    o_ref = rest[n_extra]
    if nk == 1:
        sub = o_ref.shape[1] // n_sub
        for c in range(n_sub):
            cols = slice(c * sub, (c + 1) * sub)
            acc = jnp.dot(x_ref[...], w_ref[:, cols], preferred_element_type=F32)
            epilogue(acc, extra, o_ref.at[:, cols])
    else:
        acc_ref = rest[n_extra + 1]
        k = pl.program_id(2)

        @pl.when(k == 0)
        def _():
            acc_ref[...] = jnp.zeros_like(acc_ref)

        acc_ref[...] += jnp.dot(x_ref[...], w_ref[...], preferred_element_type=F32)

        @pl.when(k == nk - 1)
        def _():
            epilogue(acc_ref[...], extra, o_ref)


def _matmul(x, w, layer, *, out_dtype, epilogue, tm=1024, tn=1024, tk=None, extra=(), name="matmul"):
    m, kdim = x.shape
    n = w.shape[2]
    tm = _tile(m, tm)
    tn = _tile(n, tn)
    tk = kdim if tk is None else _tile(kdim, tk)
    nk = kdim // tk
    n_sub = max(1, tn // 512) if nk == 1 else 1
    in_specs = [
        pl.BlockSpec((tm, tk), lambda i, j, k: (i, k)),
        pl.BlockSpec((None, tk, tn), lambda i, j, k: (layer, k, j)),
    ]
    args = [x, w]
    for arr, bshape, imap in extra:
        args.append(arr)
        in_specs.append(pl.BlockSpec(bshape, imap))
    scratch = [pltpu.VMEM((tm, tn), F32)] if nk > 1 else []
    kern = functools.partial(_mm_kernel, nk=nk, n_extra=len(extra), epilogue=epilogue, n_sub=n_sub)
    return pl.pallas_call(
        kern,
        grid=(m // tm, n // tn, nk),
        in_specs=in_specs,
        out_specs=pl.BlockSpec((tm, tn), lambda i, j, k: (i, j)),
        out_shape=jax.ShapeDtypeStruct((m, n), out_dtype),
        scratch_shapes=scratch,
        compiler_params=_cparams("parallel", "arbitrary", "arbitrary"),
        name=name,
    )(*args)


def _ep_store(acc, extra, o_ref):
    o_ref[...] = acc.astype(o_ref.dtype)


def _ep_relu2(acc, extra, o_ref):
    r = jnp.maximum(acc, 0.0)
    o_ref[...] = (r * r).astype(o_ref.dtype)


def _gelu_tanh(x):
    c = math.sqrt(2.0 / math.pi)
    return x * (0.5 * (1.0 + jnp.tanh(c * (x + 0.044715 * (x * x * x)))))


def _inproj_b_kernel(x_ref, w_ref, o_ref, *, n_raw, n_gelu, n_sub):
    j = pl.program_id(1)
    sub = o_ref.shape[1] // n_sub

    def run(act):
        for c in range(n_sub):
            acc = lax.dot_general(x_ref[...], w_ref[c * sub:(c + 1) * sub, :], _NT_DIMS,
                                  preferred_element_type=F32)
            o_ref[:, c * sub:(c + 1) * sub] = act(acc).astype(o_ref.dtype)

    @pl.when(j < n_raw)
    def _():
        run(lambda a: a)

    @pl.when((j >= n_raw) & (j < n_raw + n_gelu))
    def _():
        run(_gelu_tanh)

    @pl.when(j >= n_raw + n_gelu)
    def _():
        run(jax.nn.sigmoid)


def _inproj_b_call(x, w, layer, *, d_rnn, tm=1024, tn=1024):
    m, d = x.shape
    n = w.shape[1]
    tm = _tile(m, tm)
    tn = _tile(d_rnn, tn)
    n_sub = max(1, tn // 512)
    kern = functools.partial(_inproj_b_kernel, n_raw=d_rnn // tn, n_gelu=d_rnn // tn, n_sub=n_sub)
    return pl.pallas_call(
        kern,
        grid=(m // tm, n // tn),
        in_specs=[
            pl.BlockSpec((tm, d), lambda i, j: (i, 0)),
            pl.BlockSpec((None, tn, d), lambda i, j: (layer, j, 0)),
        ],
        out_specs=pl.BlockSpec((tm, tn), lambda i, j: (i, j)),
        out_shape=jax.ShapeDtypeStruct((m, n), BF16),
        compiler_params=_cparams("parallel", "arbitrary"),
        name="inproj_b",
    )(x, w)


def _rope_slab(hi, cos, sin, rope):
    return hi * cos + pltpu.roll(hi, rope, axis=1) * sin


def _ep_uq(acc, extra, o_ref, *, rope, scale):
    cos = extra[0][...]
    sin = extra[1][...]
    for hh in range(acc.shape[1] // HEAD_W):
        base = hh * HEAD_W
        lo = acc[:, base:base + LANES] * scale
        hi = _rope_slab(acc[:, base + LANES:base + HEAD_W], cos, sin, rope) * scale
        o_ref[:, base:base + LANES] = lo.astype(o_ref.dtype)
        o_ref[:, base + LANES:base + HEAD_W] = hi.astype(o_ref.dtype)


def _inproj_a_kernel(h_ref, w_ref, qg_ref, kvg_ref, cos_ref, sin_ref,
                     qc_ref, ckv_ref, kr_ref, ckr_ref, *, q_rank, kv_rank, rope):
    acc = lax.dot_general(h_ref[...], w_ref[...], _NT_DIMS, preferred_element_type=F32)
    q = acc[:, :q_rank]
    qn = q * lax.rsqrt(jnp.mean(q * q, axis=-1, keepdims=True) + RMS_EPS) * qg_ref[...]
    qc_ref[...] = qn.astype(BF16)
    kv = acc[:, q_rank:q_rank + kv_rank]
    kvn = kv * lax.rsqrt(jnp.mean(kv * kv, axis=-1, keepdims=True) + RMS_EPS) * kvg_ref[...]
    ckv_ref[...] = kvn
    kr = _rope_slab(acc[:, q_rank + kv_rank:], cos_ref[...], sin_ref[...], rope)
    kr_ref[...] = kr[:, :rope]
    ckr_ref[:, :kv_rank] = kvn.astype(BF16)
    ckr_ref[:, kv_rank:] = kr.astype(BF16)


def _inproj_a_call(h, w_a, layer, qg, kvg, cos_t, sin_t, *, q_rank, kv_rank, rope, tm):
    m, d = h.shape
    na = w_a.shape[1]
    nt = cos_t.shape[0] // tm
    row = lambda i: (i, 0)
    fixed = lambda i: (0, 0)
    tab = lambda i: (i % nt, 0)
    kern = functools.partial(_inproj_a_kernel, q_rank=q_rank, kv_rank=kv_rank, rope=rope)
    return pl.pallas_call(
        kern,
        grid=(m // tm,),
        in_specs=[
            pl.BlockSpec((tm, d), row),
            pl.BlockSpec((None, na, d), lambda i: (layer, 0, 0), pipeline_mode=pl.Buffered(1)),
            pl.BlockSpec((1, q_rank), fixed),
            pl.BlockSpec((1, kv_rank), fixed),
            pl.BlockSpec((tm, LANES), tab),
            pl.BlockSpec((tm, LANES), tab),
        ],
        out_specs=[
            pl.BlockSpec((tm, q_rank), row),
            pl.BlockSpec((tm, kv_rank), row),
            pl.BlockSpec((tm, rope), row),
            pl.BlockSpec((tm, kv_rank + LANES), row),
        ],
        out_shape=[
            jax.ShapeDtypeStruct((m, q_rank), BF16),
            jax.ShapeDtypeStruct((m, kv_rank), F32),
            jax.ShapeDtypeStruct((m, rope), F32),
            jax.ShapeDtypeStruct((m, kv_rank + LANES), BF16),
        ],
        compiler_params=_cparams("parallel"),
        name="inproj_a",
    )(h, w_a, qg, kvg, cos_t, sin_t)


def _dual_kernel(a_ref, b_ref, wa_ref, wb_ref, ga_ref, gb_ref, o_ref):
    ya = jnp.dot(a_ref[...], wa_ref[...], preferred_element_type=F32)
    yb = jnp.dot(b_ref[...], wb_ref[...], preferred_element_type=F32)
    o_ref[...] = (ga_ref[...].astype(F32) * ya + gb_ref[...].astype(F32) * yb).astype(o_ref.dtype)


def _dual_call(a, b, wa, wb, layer, gates_src, ga_col, gb_col, *, tm=512, tn=512):
    m, ka = a.shape
    kb = b.shape[1]
    n = wa.shape[2]
    tm = _tile(m, tm)
    tn = _tile(n, tn)
    ga_blk, gb_blk = ga_col // tn, gb_col // tn
    return pl.pallas_call(
        _dual_kernel,
        grid=(m // tm, n // tn),
        in_specs=[
            pl.BlockSpec((tm, ka), lambda i, j: (i, 0)),
            pl.BlockSpec((tm, kb), lambda i, j: (i, 0)),
            pl.BlockSpec((None, ka, tn), lambda i, j: (layer, 0, j)),
            pl.BlockSpec((None, kb, tn), lambda i, j: (layer, 0, j)),
            pl.BlockSpec((tm, tn), lambda i, j: (i, ga_blk + j)),
            pl.BlockSpec((tm, tn), lambda i, j: (i, gb_blk + j)),
        ],
        out_specs=pl.BlockSpec((tm, tn), lambda i, j: (i, j)),
        out_shape=jax.ShapeDtypeStruct((m, n), BF16),
        compiler_params=_cparams("parallel", "arbitrary"),
        name="branch_merge",
    )(a, b, wa, wb, gates_src, gates_src)


def _rnn_kernel(xr_ref, gy_ref, cw_ref, cb_ref, wa_ref, wx_ref, ba_ref, bx_ref, lam_ref, h0_ref, cp_ref,
                hb_ref, hl_ref, cn_ref, xbuf, hcar, *, ts, conv_w):
    t = pl.program_id(2)
    pad = SUBLANES
    hist = conv_w - 1

    @pl.when(t == 0)
    def _():
        xbuf[0:pad, :] = jnp.zeros((pad, xbuf.shape[1]), F32)
        xbuf[pad - hist:pad, :] = cp_ref[0, 0]
        hcar[...] = h0_ref[0, 0]

    x = xr_ref[0].astype(F32)
    xbuf[pad:pad + ts, :] = x
    xc = cb_ref[...] + cw_ref[hist:hist + 1, :] * x
    for k in range(hist):
        d = hist - k
        xc = xc + cw_ref[k:k + 1, :] * xbuf[pad - d:pad - d + ts, :]
    cn_ref[0] = xbuf[pad + ts - hist:pad + ts, :]
    xbuf[0:pad, :] = xbuf[ts:ts + pad, :]

    xb = xc.astype(BF16)
    r = jax.nn.sigmoid(jnp.dot(xb, wa_ref[0], preferred_element_type=F32) + ba_ref[...])
    ig = jax.nn.sigmoid(jnp.dot(xb, wx_ref[0], preferred_element_type=F32) + bx_ref[...])
    nl = -lam_ref[...]
    softplus = jnp.maximum(nl, 0.0) + jnp.log1p(jnp.exp(-jnp.abs(nl)))
    log_a = (-RG_C) * r * softplus
    a = jnp.exp(log_a)
    gain = jnp.sqrt(-jnp.tanh(log_a) * (a * a + 1.0))
    bv = gain * (ig * xc)

    ng = ts // SUBLANES
    a3 = a.reshape(ng, SUBLANES, a.shape[1])
    b3 = bv.reshape(ng, SUBLANES, a.shape[1])
    sub = lax.broadcasted_iota(jnp.int32, a3.shape, 1)
    d = 1
    while d < SUBLANES:
        keep = sub >= d
        b3 = jnp.where(keep, a3 * pltpu.roll(b3, d, axis=1) + b3, b3)
        a3 = jnp.where(keep, a3 * pltpu.roll(a3, d, axis=1), a3)
        d *= 2
    carry = hcar[...]
    gy = gy_ref[0].astype(F32).reshape(a3.shape)
    for g in range(0, ng, 2):
        outs = []
        for gg in (g, g + 1):
            hg = a3[gg] * carry + b3[gg]
            carry = hg[SUBLANES - 1:SUBLANES, :]
            outs.append(hg * gy[gg])
        hb_ref[0, g * SUBLANES:(g + 2) * SUBLANES, :] = jnp.concatenate(outs, axis=0).astype(BF16)
    hcar[...] = carry
    hl_ref[0] = carry


def _rnn_call(proj_b, conv_w_l, conv_b_l, wa, wx, wlayer, ba, bx, lam, h0, cpast, layer, *, d_rnn, xr_col, gy_col):
    b, s, _ = proj_b.shape
    _, nblk, bw, _ = wa.shape
    conv_w = conv_w_l.shape[0]
    hist = conv_w - 1
    assert s >= hist and s % SUBLANES == 0
    ts = _tile(s, 512)
    xr_blk, gy_blk = xr_col // bw, gy_col // bw
    vec = lambda bi, n, t: (0, n)
    kern = functools.partial(_rnn_kernel, ts=ts, conv_w=conv_w)
    return pl.pallas_call(
        kern,
        grid=(b, nblk, s // ts),
        in_specs=[
            pl.BlockSpec((1, ts, bw), lambda bi, n, t: (bi, t, xr_blk + n)),
            pl.BlockSpec((1, ts, bw), lambda bi, n, t: (bi, t, gy_blk + n)),
            pl.BlockSpec((conv_w, bw), vec),
            pl.BlockSpec((1, bw), vec),
            pl.BlockSpec((None, 1, bw, bw), lambda bi, n, t: (wlayer, n, 0, 0)),
            pl.BlockSpec((None, 1, bw, bw), lambda bi, n, t: (wlayer, n, 0, 0)),
            pl.BlockSpec((1, bw), vec),
            pl.BlockSpec((1, bw), vec),
            pl.BlockSpec((1, bw), vec),
            pl.BlockSpec((1, 1, 1, bw), lambda bi, n, t: (layer, bi, 0, n)),
            pl.BlockSpec((1, 1, hist, bw), lambda bi, n, t: (layer, bi, 0, n)),
        ],
        out_specs=[
            pl.BlockSpec((1, ts, bw), lambda bi, n, t: (bi, t, n)),
            pl.BlockSpec((1, 1, bw), lambda bi, n, t: (bi, 0, n)),
            pl.BlockSpec((1, hist, bw), lambda bi, n, t: (bi, 0, n)),
        ],
        out_shape=[
            jax.ShapeDtypeStruct((b, s, d_rnn), BF16),
            jax.ShapeDtypeStruct((b, 1, d_rnn), F32),
            jax.ShapeDtypeStruct((b, hist, d_rnn), F32),
        ],
        scratch_shapes=[pltpu.VMEM((SUBLANES + ts, bw), F32), pltpu.VMEM((1, bw), F32)],
        compiler_params=_cparams("parallel", "parallel", "arbitrary"),
        name="rglru",
    )(proj_b, proj_b, conv_w_l, conv_b_l.reshape(1, d_rnn), wa, wx, ba.reshape(1, d_rnn),
      bx.reshape(1, d_rnn), lam.reshape(1, d_rnn), h0, cpast)


def _attn_kernel(q_ref, k_ref, vt_ref, o_ref, s0, s1, p0, p1, a0, a1, m_sc, acc_sc, *, tq, v_dim, hp):
    i = pl.program_id(2)
    n = i + 1
    s_buf, p_buf, a_buf = (s0, s1), (p0, p1), (a0, a1)
    m_sc[...] = jnp.full(m_sc.shape, -jnp.inf, F32)
    acc_sc[...] = jnp.zeros(acc_sc.shape, F32)

    def kv_off(t):
        if isinstance(t, int):
            idx = i if t == 0 else t - 1
        else:
            idx = jnp.where(t == 0, i, t - 1)
        return pl.multiple_of(idx * tq, tq)

    def stage_a(t, slot, masked=False):
        off = kv_off(t)
        for hh in range(hp):
            q = q_ref[0, :, hh * HEAD_W:(hh + 1) * HEAD_W]
            k = k_ref[0, pl.ds(off, tq), hh * HEAD_W:(hh + 1) * HEAD_W]
            st = lax.dot_general(k, q, (((1,), (1,)), ((), ())), preferred_element_type=F32)
            if masked:
                kc = lax.broadcasted_iota(jnp.int32, st.shape, 0) // CHUNK
                qc = lax.broadcasted_iota(jnp.int32, st.shape, 1) // CHUNK
                st = jnp.where(kc <= qc, st, jnp.finfo(F32).min)
            s_buf[slot][hh] = st

    def stage_b(slot):
        for hh in range(hp):
            m_prev = m_sc[hh]
            m_new = jnp.maximum(m_prev, jnp.max(s_buf[slot][hh], axis=0, keepdims=True))
            m_sc[hh] = m_new
            a_buf[slot][hh] = jnp.exp2(m_prev - m_new)
            p_buf[slot][hh] = jnp.exp2(s_buf[slot][hh] - m_new).astype(BF16)

    def stage_c(t, slot):
        off = kv_off(t)
        vrows = v_dim + ONES_ROWS
        for hh in range(hp):
            vt = vt_ref[0, hh * vrows:(hh + 1) * vrows, pl.ds(off, tq)]
            acc_sc[hh] = a_buf[slot][hh] * acc_sc[hh] + jnp.dot(vt, p_buf[slot][hh], preferred_element_type=F32)

    def half_step(t, par):
        stage_a(t, par)
        stage_b(1 - par)
        stage_c(t - 2, par)

    stage_a(0, 0, masked=True)

    @pl.when(n == 1)
    def _():
        stage_b(0)
        stage_c(0, 0)

    @pl.when(n >= 2)
    def _():
        stage_a(1, 1)
        stage_b(0)

        def body(u, carry):
            t = 2 * u
            half_step(t, 0)
            half_step(t + 1, 1)
            return carry

        lax.fori_loop(1, n // 2, body, 0)

        @pl.when(n % 2 == 1)
        def _():
            half_step(n - 1, 0)
            stage_b(0)
            stage_c(n - 2, 1)
            stage_c(n - 1, 0)

        @pl.when(n % 2 == 0)
        def _():
            stage_b(1)
            stage_c(n - 2, 0)
            stage_c(n - 1, 1)

    for hh in range(hp):
        o = acc_sc[hh, 0:v_dim, :] / acc_sc[hh, v_dim:v_dim + 1, :]
        o_ref[0, :, hh * v_dim:(hh + 1) * v_dim] = o.T.astype(o_ref.dtype)


def _attn_call(q_aug, k_aug, vt, *, heads, v_dim):
    b, s, _ = q_aug.shape
    tq = _tile(s, 512)
    assert tq % CHUNK == 0
    vrows = v_dim + ONES_ROWS
    hp = _tile(heads, 2)
    kern = functools.partial(_attn_kernel, tq=tq, v_dim=v_dim, hp=hp)
    score = pltpu.VMEM((hp, tq, tq), F32)
    prob = pltpu.VMEM((hp, tq, tq), BF16)
    stat = pltpu.VMEM((hp, 1, tq), F32)
    return pl.pallas_call(
        kern,
        grid=(b, heads // hp, s // tq),
        in_specs=[
            pl.BlockSpec((1, tq, hp * HEAD_W), lambda bi, h, i: (bi, i, h)),
            pl.BlockSpec((1, s, hp * HEAD_W), lambda bi, h, i: (bi, 0, h)),
            pl.BlockSpec((1, hp * vrows, s), lambda bi, h, i: (bi, h, 0)),
        ],
        out_specs=pl.BlockSpec((1, tq, hp * v_dim), lambda bi, h, i: (bi, i, h)),
        out_shape=jax.ShapeDtypeStruct((b, s, heads * v_dim), BF16),
        scratch_shapes=[score, score, prob, prob, stat, stat, stat, pltpu.VMEM((hp, vrows, tq), F32)],
        compiler_params=_cparams("parallel", "parallel", "arbitrary"),
        name="attn_prompt",
    )(q_aug, k_aug, vt)


def _vt_expand_kernel(w_ref, x_ref, o_ref, *, hg, v_dim):
    acc = lax.dot_general(w_ref[...], x_ref[0], (((1,), (1,)), ((), ())), preferred_element_type=F32)
    for hh in range(hg):
        base = hh * (v_dim + ONES_ROWS)
        o_ref[0, base:base + v_dim, :] = acc[hh * v_dim:(hh + 1) * v_dim, :].astype(o_ref.dtype)
        o_ref[0, base + v_dim:base + v_dim + ONES_ROWS, :] = jnp.ones((ONES_ROWS, acc.shape[1]), o_ref.dtype)


def _vt_expand_call(ckr, w_vt, layer, *, v_dim):
    b, s, f = ckr.shape
    n = w_vt.shape[1]
    tr = _tile(n, 512)
    ts = _tile(s, 1024)
    hg = tr // v_dim
    rows = hg * (v_dim + ONES_ROWS)
    return pl.pallas_call(
        functools.partial(_vt_expand_kernel, hg=hg, v_dim=v_dim),
        grid=(b, s // ts, n // tr),
        in_specs=[
            pl.BlockSpec((None, tr, f), lambda bi, t, r: (layer, r, 0)),
            pl.BlockSpec((1, ts, f), lambda bi, t, r: (bi, t, 0)),
        ],
        out_specs=pl.BlockSpec((1, rows, ts), lambda bi, t, r: (bi, r, t)),
        out_shape=jax.ShapeDtypeStruct((b, n // tr * rows, s), BF16),
        compiler_params=_cparams("parallel", "parallel", "arbitrary"),
        name="vt_expand",
    )(w_vt, ckr)


def _attn_dec_kernel(q_ref, cp_ref, kp_ref, cn_ref, wuk_ref, wuv_ref, o_ref, cpb, kpb, *, hg, kv_rank, rope, s):
    g = pl.program_id(1)

    @pl.when(g == 0)
    def _():
        cpb[...] = cp_ref[0, 0].astype(BF16)
        kpb[...] = kp_ref[0, 0].astype(BF16)

    nt = (((1,), (1,)), ((), ()))
    ql, qr = [], []
    for hh in range(hg):
        base = hh * HEAD_W
        qn = q_ref[0, :, base:base + LANES]
        ql.append(jnp.dot(qn, wuk_ref[hh], preferred_element_type=F32).astype(BF16))
        qr.append(q_ref[0, :, base + LANES:base + LANES + rope])
    ql = jnp.concatenate(ql, axis=0)
    qr = jnp.concatenate(qr, axis=0)
    c_new = cn_ref[0, :, :kv_rank]
    k_new = cn_ref[0, :, kv_rank:kv_rank + rope]
    c_past = cpb[...]
    s_past = (lax.dot_general(ql, c_past, nt, preferred_element_type=F32)
              + lax.dot_general(qr, kpb[...], nt, preferred_element_type=F32))
    s_new = (lax.dot_general(ql, c_new, nt, preferred_element_type=F32)
             + lax.dot_general(qr, k_new, nt, preferred_element_type=F32))
    m = jnp.maximum(jnp.max(s_past, axis=-1, keepdims=True), jnp.max(s_new, axis=-1, keepdims=True))
    p_past = jnp.exp2(s_past - m)
    p_new = jnp.exp2(s_new - m)
    den = jnp.sum(p_past, axis=-1, keepdims=True) + jnp.sum(p_new, axis=-1, keepdims=True)
    o_lat = (jnp.dot(p_past.astype(BF16), c_past, preferred_element_type=F32)
             + jnp.dot(p_new.astype(BF16), c_new, preferred_element_type=F32)) / den
    o_lat = o_lat.astype(BF16)
    for hh in range(hg):
        oh = jnp.dot(o_lat[hh * s:(hh + 1) * s, :], wuv_ref[hh], preferred_element_type=F32)
        o_ref[0, :, hh * LANES:(hh + 1) * LANES] = oh.astype(o_ref.dtype)


def _attn_dec_call(q_aug, cache_kv, cache_kr, ckr, wuk_t, wuv_h, layer, *, heads, kv_rank, rope, v_dim):
    b, s, _ = q_aug.shape
    past = cache_kv.shape[2]
    assert past % CHUNK == 0 and s <= CHUNK and v_dim == LANES
    hg = _tile(heads, 4)
    kern = functools.partial(_attn_dec_kernel, hg=hg, kv_rank=kv_rank, rope=rope, s=s)
    return pl.pallas_call(
        kern,
        grid=(b, heads // hg),
        in_specs=[
            pl.BlockSpec((1, s, hg * HEAD_W), lambda bi, g: (bi, 0, g)),
            pl.BlockSpec((1, 1, past, kv_rank), lambda bi, g: (layer, bi, 0, 0)),
            pl.BlockSpec((1, 1, past, rope), lambda bi, g: (layer, bi, 0, 0)),
            pl.BlockSpec((1, s, kv_rank + LANES), lambda bi, g: (bi, 0, 0)),
            pl.BlockSpec((None, hg, LANES, kv_rank), lambda bi, g: (layer, g, 0, 0)),
            pl.BlockSpec((None, hg, kv_rank, v_dim), lambda bi, g: (layer, g, 0, 0)),
        ],
        out_specs=pl.BlockSpec((1, s, hg * v_dim), lambda bi, g: (bi, 0, g)),
        out_shape=jax.ShapeDtypeStruct((b, s, heads * v_dim), BF16),
        scratch_shapes=[pltpu.VMEM((past, kv_rank), BF16), pltpu.VMEM((past, rope), BF16)],
        compiler_params=_cparams("parallel", "arbitrary"),
        name="attn_sample",
    )(q_aug, cache_kv, cache_kr, ckr, wuk_t, wuv_h)


def _rotate_half_cols(w, half):
    return jnp.concatenate([-w[..., half:], w[..., :half]], axis=-1)


def _prep_weights(p, dims):
    q_rank, kv_rank, rope, heads, nope, v_dim, d_rnn, d_model = (
        dims[k] for k in ("q_rank", "kv_rank", "rope", "heads", "nope", "v_dim", "d_rnn", "d_model"))
    half = rope // 2
    depth = p["w_in"].shape[0]
    off_kr = q_rank + kv_rank
    off_rx = off_kr + rope
    w_in_t = jnp.swapaxes(p["w_in"], 1, 2)
    w_kr_t = w_in_t[:, off_kr:off_rx]
    w_kr_rot = jnp.concatenate([-w_kr_t[:, half:], w_kr_t[:, :half]], axis=1)
    w_a = jnp.concatenate([w_in_t[:, :off_rx], w_kr_rot], axis=1).astype(BF16)
    w_b = w_in_t[:, off_rx:].astype(BF16)

    wq = p["w_uq"].reshape(depth, q_rank, heads, nope + rope)
    wq_r = wq[..., nope:]
    w_uq = jnp.concatenate([wq[..., :nope], wq_r, _rotate_half_cols(wq_r, half)], axis=-1)
    w_uq = w_uq.reshape(depth, q_rank, heads * HEAD_W).astype(BF16)

    zk = jnp.zeros((depth, kv_rank, heads, HEAD_W - nope), F32)
    k_top = jnp.concatenate([p["w_uk"], zk], axis=-1)
    eye = jnp.zeros((LANES, HEAD_W), F32).at[jnp.arange(rope), nope + jnp.arange(rope)].set(1.0)
    k_bot = jnp.broadcast_to(eye[None, :, None, :], (depth, LANES, heads, HEAD_W))
    k_w = jnp.concatenate([k_top, k_bot], axis=1).reshape(depth, kv_rank + LANES, heads * HEAD_W)
    v_w = jnp.concatenate([p["w_uv"], jnp.zeros((depth, LANES, heads, v_dim), F32)], axis=1)
    w_vt = jnp.transpose(v_w, (0, 2, 3, 1)).reshape(depth, heads * v_dim, kv_rank + LANES).astype(BF16)

    return dict(
        w_a=w_a, w_b=w_b, w_uq=w_uq, w_k=k_w.astype(BF16), w_vt=w_vt,
        wuk_t=jnp.transpose(p["w_uk"], (0, 2, 3, 1)).astype(BF16),
        wuv_h=jnp.transpose(p["w_uv"], (0, 2, 1, 3)).astype(BF16),
        w_o_attn=p["w_o_attn"].astype(BF16), w_o_rnn=p["w_o_rnn"].astype(BF16),
        w_out=p["w_out"].astype(BF16), w_up=p["w_up"].astype(BF16), w_down=p["w_down"].astype(BF16),
        w_rg_a=p["w_rg_a"].astype(BF16), w_rg_x=p["w_rg_x"].astype(BF16),
    )


def _rope_tables(past, s, rope, rows):
    half = rope // 2
    pos = past + jnp.arange(s, dtype=jnp.int32)
    inv = ROPE_THETA ** (-jnp.arange(half, dtype=F32) / half)
    ang = pos.astype(F32)[:, None] * inv[None, :]
    zeros = jnp.zeros((s, LANES - rope), F32)
    cos_t = jnp.concatenate([jnp.cos(ang), jnp.cos(ang), zeros], axis=-1)
    sin_t = jnp.concatenate([jnp.sin(ang), jnp.sin(ang), zeros], axis=-1)
    if rows > s:
        cos_t = jnp.tile(cos_t, (rows // s, 1))
        sin_t = jnp.tile(sin_t, (rows // s, 1))
    return cos_t, sin_t


def _run_trunk(x, mod, p, w, dims, *, cache_kv=None, cache_kr=None, h_past=None, conv_past=None):
    b, s, d = x.shape
    m = b * s
    depth = p["w_in"].shape[0]
    q_rank, kv_rank, rope, heads, nope, v_dim, d_rnn = (
        dims[k] for k in ("q_rank", "kv_rank", "rope", "heads", "nope", "v_dim", "d_rnn"))
    alpha = (2 * depth) ** 0.25
    scale = (nope + rope) ** -0.5 * math.log2(math.e)
    has_past = cache_kv is not None
    past = cache_kv.shape[2] if has_past else 0
    hist = p["conv_w"].shape[1] - 1
    if not has_past:
        h_past = jnp.zeros((1, b, 1, d_rnn), F32)
        conv_past = jnp.zeros((1, b, hist, d_rnn), F32)
    else:
        h_past = h_past.reshape(depth, b, 1, d_rnn)

    tm = _tile(m, 1024)
    assert tm % s == 0 or s % tm == 0
    cos_t, sin_t = _rope_tables(past, s, rope, max(s, tm))
    nt = cos_t.shape[0] // tm
    tab_spec = lambda arr: (arr, (tm, LANES), lambda i, j, k: (i % nt, 0))

    ckv_l, kr_l, h_l, conv_l = [], [], [], []
    x, h = _ln_mod_call(x, p["ln_in_g"], p["ln_in_b"], nmod=mod, mod_layer=0, sc_row=1, sh_row=0)
    tm_a = _tile(m, 512)
    for l in range(depth):
        h2 = h.reshape(m, d)
        q_c, ckv, kr, ckr = _inproj_a_call(
            h2, w["w_a"], l, p["q_norm_g"][l].reshape(1, q_rank), p["kv_norm_g"][l].reshape(1, kv_rank),
            cos_t, sin_t, q_rank=q_rank, kv_rank=kv_rank, rope=rope, tm=tm_a)
        n_b = w["w_b"].shape[1]
        proj_b = _inproj_b_call(h2, w["w_b"], l, d_rnn=d_rnn)
        q_aug = _matmul(
            q_c, w["w_uq"], l, out_dtype=BF16, name="uq", extra=[tab_spec(cos_t), tab_spec(sin_t)],
            epilogue=functools.partial(_ep_uq, rope=rope, scale=scale))
        q3 = q_aug.reshape(b, s, heads * HEAD_W)
        if has_past:
            o = _attn_dec_call(q3, cache_kv, cache_kr, ckr.reshape(b, s, kv_rank + LANES),
                               w["wuk_t"], w["wuv_h"], l,
                               heads=heads, kv_rank=kv_rank, rope=rope, v_dim=v_dim)
        else:
            k_aug = _matmul(ckr, w["w_k"], l, out_dtype=BF16, name="k_expand", epilogue=_ep_store)
            vt = _vt_expand_call(ckr.reshape(b, s, kv_rank + LANES), w["w_vt"], l, v_dim=v_dim)
            o = _attn_call(q3, k_aug.reshape(b, s, heads * HEAD_W), vt, heads=heads, v_dim=v_dim)
        hb, h_last, conv_n = _rnn_call(
            proj_b.reshape(b, s, n_b), p["conv_w"][l], p["conv_b"][l], w["w_rg_a"], w["w_rg_x"], l,
            p["b_rg_a"][l].reshape(-1), p["b_rg_x"][l].reshape(-1), p["rg_lambda"][l],
            h_past, conv_past, l if has_past else 0, d_rnn=d_rnn, xr_col=0, gy_col=d_rnn)
        merged = _dual_call(o.reshape(m, heads * v_dim), hb.reshape(m, d_rnn), w["w_o_attn"], w["w_o_rnn"], l,
                            proj_b, 2 * d_rnn, 2 * d_rnn + d)
        mix = _matmul(merged, w["w_out"], l, out_dtype=BF16, name="w_out", epilogue=_ep_store)
        x, h = _ln_mod_call(x, p["ln_mix_g"][l], p["ln_mix_b"][l], res=mix.reshape(b, s, d), gmod=mod,
                            gate_layer=l, gate_row=2, nmod=mod, mod_layer=l, sc_row=4, sh_row=3, alpha=alpha)
        up = _matmul(h.reshape(m, d), w["w_up"], l, out_dtype=BF16, name="ffn_up", epilogue=_ep_relu2)
        ff = _matmul(up, w["w_down"], l, out_dtype=BF16, tk=2048, name="ffn_down", epilogue=_ep_store)
        last = l == depth - 1
        x, h = _ln_mod_call(x, p["ln_ffn_g"][l], p["ln_ffn_b"][l], res=ff.reshape(b, s, d), gmod=mod,
                            gate_layer=l, gate_row=5, nmod=None if last else mod,
                            mod_layer=0 if last else l + 1, sc_row=1, sh_row=0, alpha=alpha)
        ckv_l.append(ckv.reshape(b, s, kv_rank))
        kr_l.append(kr.reshape(b, s, rope))
        h_l.append(h_last.reshape(b, d_rnn))
        conv_l.append(conv_n)
    return x, jnp.stack(ckv_l), jnp.stack(kr_l), jnp.stack(h_l), jnp.stack(conv_l)


def kernel(x_prompt, x_sample, cache_kv_latent, cache_k_rope, state_rglru_h, state_conv, c_prompt, c_sample, ln_in_g, ln_in_b, w_in, q_norm_g, w_uq, kv_norm_g, w_uk, w_uv, w_o_attn, conv_w, conv_b, w_rg_a, b_rg_a, w_rg_x, b_rg_x, rg_lambda, w_o_rnn, w_out, w_mod, b_mod, ln_mix_g, ln_mix_b, w_up, w_down, ln_ffn_g, ln_ffn_b):
    p = dict(ln_in_g=ln_in_g, ln_in_b=ln_in_b, w_in=w_in, q_norm_g=q_norm_g, w_uq=w_uq, kv_norm_g=kv_norm_g,
             w_uk=w_uk, w_uv=w_uv, w_o_attn=w_o_attn, conv_w=conv_w, conv_b=conv_b, w_rg_a=w_rg_a,
             b_rg_a=b_rg_a, w_rg_x=w_rg_x, b_rg_x=b_rg_x, rg_lambda=rg_lambda, w_o_rnn=w_o_rnn, w_out=w_out,
             w_mod=w_mod, b_mod=b_mod, ln_mix_g=ln_mix_g, ln_mix_b=ln_mix_b, w_up=w_up, w_down=w_down,
             ln_ffn_g=ln_ffn_g, ln_ffn_b=ln_ffn_b)
    depth, d_model, _ = w_in.shape
    _, kv_rank, heads, nope = w_uk.shape
    dims = dict(d_model=d_model, q_rank=q_norm_g.shape[-1], kv_rank=kv_rank, heads=heads, nope=nope,
                rope=w_uq.shape[-1] // heads - nope, v_dim=w_uv.shape[-1], d_rnn=conv_w.shape[-1])
    assert dims["nope"] == LANES and dims["rope"] == LANES // 2 and dims["v_dim"] == LANES
    assert dims["d_rnn"] == d_model
    n_mod = w_mod.shape[-1] // d_model

    bp, bs = x_prompt.shape[0], x_sample.shape[0]
    rows = -(-(bp + bs) // SUBLANES) * SUBLANES
    c_all = jnp.concatenate([c_prompt, c_sample, jnp.zeros((rows - bp - bs, d_model), F32)], axis=0)
    mod_all = _mod_call(c_all, w_mod, b_mod).reshape(depth, rows, n_mod, d_model)
    mod_p = mod_all[:, :bp]
    mod_s = mod_all[:, bp:bp + bs]

    w = _prep_weights(p, dims)
    y_p, ckv_p, kr_p, h_p, conv_p = _run_trunk(x_prompt, mod_p, p, w, dims)
    y_s, ckv_s, kr_s, h_s, conv_s = _run_trunk(
        x_sample, mod_s, p, w, dims, cache_kv=cache_kv_latent, cache_kr=cache_k_rope,
        h_past=state_rglru_h, conv_past=state_conv)
    return (y_p, y_s, ckv_p, kr_p, h_p, conv_p, ckv_s, kr_s, h_s, conv_s)
```

```python
import functools
import math

import jax
import jax.numpy as jnp
from jax import lax
from jax.experimental import pallas as pl
from jax.experimental.pallas import tpu as pltpu

F32 = jnp.float32
BF16 = jnp.bfloat16

CHUNK = 64
ROPE_THETA = 10000.0
RG_C = 8.0
LN_EPS = 1e-5
RMS_EPS = 1e-6

LANES = 128
SUBLANES = 8
HEAD_W = 2 * LANES
ONES_ROWS = 2 * SUBLANES
VMEM_LIMIT_BYTES = 56 * 1024 * 1024


def _cparams(*sem):
    return pltpu.CompilerParams(dimension_semantics=sem, vmem_limit_bytes=VMEM_LIMIT_BYTES)


def _tile(n, pref):
    if n <= pref:
        return n
    t = pref
    while n % t:
        t //= 2
    return t


def _mod_kernel(c_ref, w_ref, b_ref, o_ref):
    c = c_ref[...]
    cm = (c * jax.nn.sigmoid(c)).astype(BF16)
    o_ref[0] = jnp.dot(cm, w_ref[0].astype(BF16), preferred_element_type=F32) + b_ref[0]


def _mod_call(c_all, w_mod, b_mod):
    depth, d, n = w_mod.shape
    rows = c_all.shape[0]
    tn = _tile(n, 512)
    return pl.pallas_call(
        _mod_kernel,
        grid=(depth, n // tn),
        in_specs=[
            pl.BlockSpec((rows, d), lambda l, j: (0, 0)),
            pl.BlockSpec((1, d, tn), lambda l, j: (l, 0, j)),
            pl.BlockSpec((1, 1, tn), lambda l, j: (l, 0, j)),
        ],
        out_specs=pl.BlockSpec((1, rows, tn), lambda l, j: (l, 0, j)),
        out_shape=jax.ShapeDtypeStruct((depth, rows, n), F32),
        compiler_params=_cparams("parallel", "arbitrary"),
        name="mod",
    )(c_all, w_mod, b_mod.reshape(depth, 1, n))


def _ln_mod_kernel(*refs, has_res, has_mod, gate_row, sc_row, sh_row, alpha):
    it = iter(refs)
    x_ref = next(it)
    if has_res:
        y_ref = next(it)
        gmod_ref = next(it)
    g_ref = next(it)
    b_ref = next(it)
    if has_mod:
        nmod_ref = next(it)
    xo_ref = next(it)
    if has_mod:
        h_ref = next(it)

    x = x_ref[...]
    if has_res:
        gate = gmod_ref[0, :, gate_row:gate_row + 1, :]
        x = alpha * x + (1.0 + gate) * y_ref[...]
    mu = jnp.mean(x, axis=-1, keepdims=True)
    xc = x - mu
    var = jnp.mean(xc * xc, axis=-1, keepdims=True)
    y = xc * lax.rsqrt(var + LN_EPS) * g_ref[...] + b_ref[...]
    xo_ref[...] = y
    if has_mod:
        sc = nmod_ref[0, :, sc_row:sc_row + 1, :]
        sh = nmod_ref[0, :, sh_row:sh_row + 1, :]
        h_ref[...] = (y * (1.0 + sc) + sh).astype(BF16)


def _ln_mod_call(x, ln_g, ln_b, *, res=None, gmod=None, gate_layer=0, gate_row=0,
                 nmod=None, mod_layer=0, sc_row=0, sh_row=0, alpha=1.0):
    b, s, d = x.shape
    rows = _tile(s, 256)
    nb = _tile(b, max(1, 256 // rows))
    grid = (b // nb, s // rows)
    blk = pl.BlockSpec((nb, rows, d), lambda i, j: (i, j, 0))
    vec = pl.BlockSpec((1, d), lambda i, j: (0, 0))
    has_res = res is not None
    has_mod = nmod is not None
    args, specs = [x], [blk]
    if has_res:
        n_mod = gmod.shape[2]
        args += [res, gmod]
        specs += [blk, pl.BlockSpec((1, nb, n_mod, d), lambda i, j: (gate_layer, i, 0, 0))]
    args += [ln_g.reshape(1, d), ln_b.reshape(1, d)]
    specs += [vec, vec]
    out_shape = [jax.ShapeDtypeStruct((b, s, d), F32)]
    out_specs = [blk]
    if has_mod:
        n_mod = nmod.shape[2]
        args += [nmod]
        specs += [pl.BlockSpec((1, nb, n_mod, d), lambda i, j: (mod_layer, i, 0, 0))]
        out_shape += [jax.ShapeDtypeStruct((b, s, d), BF16)]
        out_specs += [blk]
    kern = functools.partial(_ln_mod_kernel, has_res=has_res, has_mod=has_mod, gate_row=gate_row,
                             sc_row=sc_row, sh_row=sh_row, alpha=alpha)
    outs = pl.pallas_call(
        kern, grid=grid, in_specs=specs, out_specs=out_specs, out_shape=out_shape,
        compiler_params=_cparams("parallel", "parallel"), name="ln_mod",
    )(*args)
    return outs if has_mod else (outs[0], None)


_NT_DIMS = (((1,), (1,)), ((), ()))


def _mm_kernel(x_ref, w_ref, *rest, nk, n_extra, epilogue, n_sub):
    extra = rest[:n_extra]
    o_ref = rest[n_extra]
    if nk == 1:
        sub = o_ref.shape[1] // n_sub
        for c in range(n_sub):
            cols = slice(c * sub, (c + 1) * sub)
            acc = jnp.dot(x_ref[...], w_ref[:, cols], preferred_element_type=F32)
            epilogue(acc, extra, o_ref.at[:, cols])
    else:
        acc_ref = rest[n_extra + 1]
        k = pl.program_id(2)

        @pl.when(k == 0)
        def _():
            acc_ref[...] = jnp.zeros_like(acc_ref)

        acc_ref[...] += jnp.dot(x_ref[...], w_ref[...], preferred_element_type=F32)

        @pl.when(k == nk - 1)
        def _():
            epilogue(acc_ref[...], extra, o_ref)


def _matmul(x, w, layer, *, out_dtype, epilogue, tm=1024, tn=1024, tk=None, extra=(), name="matmul"):
    m, kdim = x.shape
    n = w.shape[2]
    tm = _tile(m, tm)
    tn = _tile(n, tn)
    tk = kdim if tk is None else _tile(kdim, tk)
    nk = kdim // tk
    n_sub = max(1, tn // 512) if nk == 1 else 1
    in_specs = [
        pl.BlockSpec((tm, tk), lambda i, j, k: (i, k)),
        pl.BlockSpec((None, tk, tn), lambda i, j, k: (layer, k, j)),
    ]
    args = [x, w]
    for arr, bshape, imap in extra:
        args.append(arr)
        in_specs.append(pl.BlockSpec(bshape, imap))
    scratch = [pltpu.VMEM((tm, tn), F32)] if nk > 1 else []
    kern = functools.partial(_mm_kernel, nk=nk, n_extra=len(extra), epilogue=epilogue, n_sub=n_sub)
    return pl.pallas_call(
        kern,
        grid=(m // tm, n // tn, nk),
        in_specs=in_specs,
        out_specs=pl.BlockSpec((tm, tn), lambda i, j, k: (i, j)),
        out_shape=jax.ShapeDtypeStruct((m, n), out_dtype),
        scratch_shapes=scratch,
        compiler_params=_cparams("parallel", "arbitrary", "arbitrary"),
        name=name,
    )(*args)


def _ep_store(acc, extra, o_ref):
    o_ref[...] = acc.astype(o_ref.dtype)


def _ep_relu2(acc, extra, o_ref):
    r = jnp.maximum(acc, 0.0)
    o_ref[...] = (r * r).astype(o_ref.dtype)


def _gelu_tanh(x):
    c = math.sqrt(2.0 / math.pi)
    return x * (0.5 * (1.0 + jnp.tanh(c * (x + 0.044715 * (x * x * x)))))


def _inproj_b_kernel(x_ref, w_ref, o_ref, *, n_raw, n_gelu, n_sub):
    j = pl.program_id(1)
    sub = o_ref.shape[1] // n_sub

    def run(act):
        for c in range(n_sub):
            acc = lax.dot_general(x_ref[...], w_ref[c * sub:(c + 1) * sub, :], _NT_DIMS,
                                  preferred_element_type=F32)
            o_ref[:, c * sub:(c + 1) * sub] = act(acc).astype(o_ref.dtype)

    @pl.when(j < n_raw)
    def _():
        run(lambda a: a)

    @pl.when((j >= n_raw) & (j < n_raw + n_gelu))
    def _():
        run(_gelu_tanh)

    @pl.when(j >= n_raw + n_gelu)
    def _():
        run(jax.nn.sigmoid)


def _inproj_b_call(x, w, layer, *, d_rnn, tm=1024, tn=1024):
    m, d = x.shape
    n = w.shape[1]
    tm = _tile(m, tm)
    tn = _tile(d_rnn, tn)
    n_sub = max(1, tn // 512)
    kern = functools.partial(_inproj_b_kernel, n_raw=d_rnn // tn, n_gelu=d_rnn // tn, n_sub=n_sub)
    return pl.pallas_call(
        kern,
        grid=(m // tm, n // tn),
        in_specs=[
            pl.BlockSpec((tm, d), lambda i, j: (i, 0)),
            pl.BlockSpec((None, tn, d), lambda i, j: (layer, j, 0)),
        ],
        out_specs=pl.BlockSpec((tm, tn), lambda i, j: (i, j)),
        out_shape=jax.ShapeDtypeStruct((m, n), BF16),
        compiler_params=_cparams("parallel", "arbitrary"),
        name="inproj_b",
    )(x, w)


def _rope_slab(hi, cos, sin, rope):
    return hi * cos + pltpu.roll(hi, rope, axis=1) * sin


def _ep_uq(acc, extra, o_ref, *, rope, scale):
    cos = extra[0][...]
    sin = extra[1][...]
    for hh in range(acc.shape[1] // HEAD_W):
        base = hh * HEAD_W
        lo = acc[:, base:base + LANES] * scale
        hi = _rope_slab(acc[:, base + LANES:base + HEAD_W], cos, sin, rope) * scale
        o_ref[:, base:base + LANES] = lo.astype(o_ref.dtype)
        o_ref[:, base + LANES:base + HEAD_W] = hi.astype(o_ref.dtype)


def _inproj_a_kernel(h_ref, w_ref, qg_ref, kvg_ref, cos_ref, sin_ref,
                     qc_ref, ckv_ref, kr_ref, ckr_ref, *, q_rank, kv_rank, rope):
    acc = lax.dot_general(h_ref[...], w_ref[...], _NT_DIMS, preferred_element_type=F32)
    q = acc[:, :q_rank]
    qn = q * lax.rsqrt(jnp.mean(q * q, axis=-1, keepdims=True) + RMS_EPS) * qg_ref[...]
    qc_ref[...] = qn.astype(BF16)
    kv = acc[:, q_rank:q_rank + kv_rank]
    kvn = kv * lax.rsqrt(jnp.mean(kv * kv, axis=-1, keepdims=True) + RMS_EPS) * kvg_ref[...]
    ckv_ref[...] = kvn
    kr = _rope_slab(acc[:, q_rank + kv_rank:], cos_ref[...], sin_ref[...], rope)
    kr_ref[...] = kr[:, :rope]
    ckr_ref[:, :kv_rank] = kvn.astype(BF16)
    ckr_ref[:, kv_rank:] = kr.astype(BF16)


def _inproj_a_call(h, w_a, layer, qg, kvg, cos_t, sin_t, *, q_rank, kv_rank, rope, tm):
    m, d = h.shape
    na = w_a.shape[1]
    nt = cos_t.shape[0] // tm
    row = lambda i: (i, 0)
    fixed = lambda i: (0, 0)
    tab = lambda i: (i % nt, 0)
    kern = functools.partial(_inproj_a_kernel, q_rank=q_rank, kv_rank=kv_rank, rope=rope)
    return pl.pallas_call(
        kern,
        grid=(m // tm,),
        in_specs=[
            pl.BlockSpec((tm, d), row),
            pl.BlockSpec((None, na, d), lambda i: (layer, 0, 0), pipeline_mode=pl.Buffered(1)),
            pl.BlockSpec((1, q_rank), fixed),
            pl.BlockSpec((1, kv_rank), fixed),
            pl.BlockSpec((tm, LANES), tab),
            pl.BlockSpec((tm, LANES), tab),
        ],
        out_specs=[
            pl.BlockSpec((tm, q_rank), row),
            pl.BlockSpec((tm, kv_rank), row),
            pl.BlockSpec((tm, rope), row),
            pl.BlockSpec((tm, kv_rank + LANES), row),
        ],
        out_shape=[
            jax.ShapeDtypeStruct((m, q_rank), BF16),
            jax.ShapeDtypeStruct((m, kv_rank), F32),
            jax.ShapeDtypeStruct((m, rope), F32),
            jax.ShapeDtypeStruct((m, kv_rank + LANES), BF16),
        ],
        compiler_params=_cparams("parallel"),
        name="inproj_a",
    )(h, w_a, qg, kvg, cos_t, sin_t)


def _dual_kernel(a_ref, b_ref, wa_ref, wb_ref, ga_ref, gb_ref, o_ref):
    ya = jnp.dot(a_ref[...], wa_ref[...], preferred_element_type=F32)
    yb = jnp.dot(b_ref[...], wb_ref[...], preferred_element_type=F32)
    o_ref[...] = (ga_ref[...].astype(F32) * ya + gb_ref[...].astype(F32) * yb).astype(o_ref.dtype)


def _dual_call(a, b, wa, wb, layer, gates_src, ga_col, gb_col, *, tm=512, tn=512):
    m, ka = a.shape
    kb = b.shape[1]
    n = wa.shape[2]
    tm = _tile(m, tm)
    tn = _tile(n, tn)
    ga_blk, gb_blk = ga_col // tn, gb_col // tn
    return pl.pallas_call(
        _dual_kernel,
        grid=(m // tm, n // tn),
        in_specs=[
            pl.BlockSpec((tm, ka), lambda i, j: (i, 0)),
            pl.BlockSpec((tm, kb), lambda i, j: (i, 0)),
            pl.BlockSpec((None, ka, tn), lambda i, j: (layer, 0, j)),
            pl.BlockSpec((None, kb, tn), lambda i, j: (layer, 0, j)),
            pl.BlockSpec((tm, tn), lambda i, j: (i, ga_blk + j)),
            pl.BlockSpec((tm, tn), lambda i, j: (i, gb_blk + j)),
        ],
        out_specs=pl.BlockSpec((tm, tn), lambda i, j: (i, j)),
        out_shape=jax.ShapeDtypeStruct((m, n), BF16),
        compiler_params=_cparams("parallel", "arbitrary"),
        name="branch_merge",
    )(a, b, wa, wb, gates_src, gates_src)


def _rnn_kernel(xr_ref, gy_ref, cw_ref, cb_ref, wa_ref, wx_ref, ba_ref, bx_ref, lam_ref, h0_ref, cp_ref,
                hb_ref, hl_ref, cn_ref, xbuf, hcar, *, ts, conv_w):
    t = pl.program_id(2)
    pad = SUBLANES
    hist = conv_w - 1

    @pl.when(t == 0)
    def _():
        xbuf[0:pad, :] = jnp.zeros((pad, xbuf.shape[1]), F32)
        xbuf[pad - hist:pad, :] = cp_ref[0, 0]
        hcar[...] = h0_ref[0, 0]

    x = xr_ref[0].astype(F32)
    xbuf[pad:pad + ts, :] = x
    xc = cb_ref[...] + cw_ref[hist:hist + 1, :] * x
    for k in range(hist):
        d = hist - k
        xc = xc + cw_ref[k:k + 1, :] * xbuf[pad - d:pad - d + ts, :]
    cn_ref[0] = xbuf[pad + ts - hist:pad + ts, :]
    xbuf[0:pad, :] = xbuf[ts:ts + pad, :]

    xb = xc.astype(BF16)
    r = jax.nn.sigmoid(jnp.dot(xb, wa_ref[0], preferred_element_type=F32) + ba_ref[...])
    ig = jax.nn.sigmoid(jnp.dot(xb, wx_ref[0], preferred_element_type=F32) + bx_ref[...])
    nl = -lam_ref[...]
    softplus = jnp.maximum(nl, 0.0) + jnp.log1p(jnp.exp(-jnp.abs(nl)))
    log_a = (-RG_C) * r * softplus
    a = jnp.exp(log_a)
    gain = jnp.sqrt(-jnp.tanh(log_a) * (a * a + 1.0))
    bv = gain * (ig * xc)

    ng = ts // SUBLANES
    a3 = a.reshape(ng, SUBLANES, a.shape[1])
    b3 = bv.reshape(ng, SUBLANES, a.shape[1])
    sub = lax.broadcasted_iota(jnp.int32, a3.shape, 1)
    d = 1
    while d < SUBLANES:
        keep = sub >= d
        b3 = jnp.where(keep, a3 * pltpu.roll(b3, d, axis=1) + b3, b3)
        a3 = jnp.where(keep, a3 * pltpu.roll(a3, d, axis=1), a3)
        d *= 2
    carry = hcar[...]
    gy = gy_ref[0].astype(F32).reshape(a3.shape)
    for g in range(0, ng, 2):
        outs = []
        for gg in (g, g + 1):
            hg = a3[gg] * carry + b3[gg]
            carry = hg[SUBLANES - 1:SUBLANES, :]
            outs.append(hg * gy[gg])
        hb_ref[0, g * SUBLANES:(g + 2) * SUBLANES, :] = jnp.concatenate(outs, axis=0).astype(BF16)
    hcar[...] = carry
    hl_ref[0] = carry


def _rnn_call(proj_b, conv_w_l, conv_b_l, wa, wx, wlayer, ba, bx, lam, h0, cpast, layer, *, d_rnn, xr_col, gy_col):
    b, s, _ = proj_b.shape
    _, nblk, bw, _ = wa.shape
    conv_w = conv_w_l.shape[0]
    hist = conv_w - 1
    assert s >= hist and s % SUBLANES == 0
    ts = _tile(s, 512)
    xr_blk, gy_blk = xr_col // bw, gy_col // bw
    vec = lambda bi, n, t: (0, n)
    kern = functools.partial(_rnn_kernel, ts=ts, conv_w=conv_w)
    return pl.pallas_call(
        kern,
        grid=(b, nblk, s // ts),
        in_specs=[
            pl.BlockSpec((1, ts, bw), lambda bi, n, t: (bi, t, xr_blk + n)),
            pl.BlockSpec((1, ts, bw), lambda bi, n, t: (bi, t, gy_blk + n)),
            pl.BlockSpec((conv_w, bw), vec),
            pl.BlockSpec((1, bw), vec),
            pl.BlockSpec((None, 1, bw, bw), lambda bi, n, t: (wlayer, n, 0, 0)),
            pl.BlockSpec((None, 1, bw, bw), lambda bi, n, t: (wlayer, n, 0, 0)),
            pl.BlockSpec((1, bw), vec),
            pl.BlockSpec((1, bw), vec),
            pl.BlockSpec((1, bw), vec),
            pl.BlockSpec((1, 1, 1, bw), lambda bi, n, t: (layer, bi, 0, n)),
            pl.BlockSpec((1, 1, hist, bw), lambda bi, n, t: (layer, bi, 0, n)),
        ],
        out_specs=[
            pl.BlockSpec((1, ts, bw), lambda bi, n, t: (bi, t, n)),
            pl.BlockSpec((1, 1, bw), lambda bi, n, t: (bi, 0, n)),
            pl.BlockSpec((1, hist, bw), lambda bi, n, t: (bi, 0, n)),
        ],
        out_shape=[
            jax.ShapeDtypeStruct((b, s, d_rnn), BF16),
            jax.ShapeDtypeStruct((b, 1, d_rnn), F32),
            jax.ShapeDtypeStruct((b, hist, d_rnn), F32),
        ],
        scratch_shapes=[pltpu.VMEM((SUBLANES + ts, bw), F32), pltpu.VMEM((1, bw), F32)],
        compiler_params=_cparams("parallel", "parallel", "arbitrary"),
        name="rglru",
    )(proj_b, proj_b, conv_w_l, conv_b_l.reshape(1, d_rnn), wa, wx, ba.reshape(1, d_rnn),
      bx.reshape(1, d_rnn), lam.reshape(1, d_rnn), h0, cpast)


def _attn_kernel(q_ref, k_ref, vt_ref, o_ref, s0, s1, p0, p1, a0, a1, m_sc, acc_sc, *, tq, v_dim, hp):
    i = pl.program_id(2)
    n = i + 1
    s_buf, p_buf, a_buf = (s0, s1), (p0, p1), (a0, a1)
    m_sc[...] = jnp.full(m_sc.shape, -jnp.inf, F32)
    acc_sc[...] = jnp.zeros(acc_sc.shape, F32)

    def kv_off(t):
        if isinstance(t, int):
            idx = i if t == 0 else t - 1
        else:
            idx = jnp.where(t == 0, i, t - 1)
        return pl.multiple_of(idx * tq, tq)

    def stage_a(t, slot, masked=False):
        off = kv_off(t)
        for hh in range(hp):
            q = q_ref[0, :, hh * HEAD_W:(hh + 1) * HEAD_W]
            k = k_ref[0, pl.ds(off, tq), hh * HEAD_W:(hh + 1) * HEAD_W]
            st = lax.dot_general(k, q, (((1,), (1,)), ((), ())), preferred_element_type=F32)
            if masked:
                kc = lax.broadcasted_iota(jnp.int32, st.shape, 0) // CHUNK
                qc = lax.broadcasted_iota(jnp.int32, st.shape, 1) // CHUNK
                st = jnp.where(kc <= qc, st, jnp.finfo(F32).min)
            s_buf[slot][hh] = st

    def stage_b(slot):
        for hh in range(hp):
            m_prev = m_sc[hh]
            m_new = jnp.maximum(m_prev, jnp.max(s_buf[slot][hh], axis=0, keepdims=True))
            m_sc[hh] = m_new
            a_buf[slot][hh] = jnp.exp2(m_prev - m_new)
            p_buf[slot][hh] = jnp.exp2(s_buf[slot][hh] - m_new).astype(BF16)

    def stage_c(t, slot):
        off = kv_off(t)
        vrows = v_dim + ONES_ROWS
        for hh in range(hp):
            vt = vt_ref[0, hh * vrows:(hh + 1) * vrows, pl.ds(off, tq)]
            acc_sc[hh] = a_buf[slot][hh] * acc_sc[hh] + jnp.dot(vt, p_buf[slot][hh], preferred_element_type=F32)

    def half_step(t, par):
        stage_a(t, par)
        stage_b(1 - par)
        stage_c(t - 2, par)

    stage_a(0, 0, masked=True)

    @pl.when(n == 1)
    def _():
        stage_b(0)
        stage_c(0, 0)

    @pl.when(n >= 2)
    def _():
        stage_a(1, 1)
        stage_b(0)

        def pair(u):
            half_step(2 * u, 0)
            half_step(2 * u + 1, 1)

        def body(v, carry):
            pair(1 + 2 * v)
            pair(2 + 2 * v)
            return carry

        n_pairs = n // 2 - 1
        lax.fori_loop(0, n_pairs // 2, body, 0)

        @pl.when(n_pairs % 2 == 1)
        def _():
            pair(n_pairs)

        @pl.when(n % 2 == 1)
        def _():
            half_step(n - 1, 0)
            stage_b(0)
            stage_c(n - 2, 1)
            stage_c(n - 1, 0)

        @pl.when(n % 2 == 0)
        def _():
            stage_b(1)
            stage_c(n - 2, 0)
            stage_c(n - 1, 1)

    for hh in range(hp):
        o = acc_sc[hh, 0:v_dim, :] / acc_sc[hh, v_dim:v_dim + 1, :]
        o_ref[0, :, hh * v_dim:(hh + 1) * v_dim] = o.T.astype(o_ref.dtype)


def _attn_call(q_aug, k_aug, vt, *, heads, v_dim):
    b, s, _ = q_aug.shape
    tq = _tile(s, 512)
    assert tq % CHUNK == 0
    vrows = v_dim + ONES_ROWS
    hp = _tile(heads, 2)
    kern = functools.partial(_attn_kernel, tq=tq, v_dim=v_dim, hp=hp)
    score = pltpu.VMEM((hp, tq, tq), F32)
    prob = pltpu.VMEM((hp, tq, tq), BF16)
    stat = pltpu.VMEM((hp, 1, tq), F32)
    return pl.pallas_call(
        kern,
        grid=(b, heads // hp, s // tq),
        in_specs=[
            pl.BlockSpec((1, tq, hp * HEAD_W), lambda bi, h, i: (bi, i, h)),
            pl.BlockSpec((1, s, hp * HEAD_W), lambda bi, h, i: (bi, 0, h)),
            pl.BlockSpec((1, hp * vrows, s), lambda bi, h, i: (bi, h, 0)),
        ],
        out_specs=pl.BlockSpec((1, tq, hp * v_dim), lambda bi, h, i: (bi, i, h)),
        out_shape=jax.ShapeDtypeStruct((b, s, heads * v_dim), BF16),
        scratch_shapes=[score, score, prob, prob, stat, stat, stat, pltpu.VMEM((hp, vrows, tq), F32)],
        compiler_params=_cparams("parallel", "parallel", "arbitrary"),
        name="attn_prompt",
    )(q_aug, k_aug, vt)


def _vt_expand_kernel(w_ref, x_ref, o_ref, *, hg, v_dim):
    acc = lax.dot_general(w_ref[...], x_ref[0], (((1,), (1,)), ((), ())), preferred_element_type=F32)
    for hh in range(hg):
        base = hh * (v_dim + ONES_ROWS)
        o_ref[0, base:base + v_dim, :] = acc[hh * v_dim:(hh + 1) * v_dim, :].astype(o_ref.dtype)
        o_ref[0, base + v_dim:base + v_dim + ONES_ROWS, :] = jnp.ones((ONES_ROWS, acc.shape[1]), o_ref.dtype)


def _vt_expand_call(ckr, w_vt, layer, *, v_dim):
    b, s, f = ckr.shape
    n = w_vt.shape[1]
    tr = _tile(n, 512)
    ts = _tile(s, 1024)
    hg = tr // v_dim
    rows = hg * (v_dim + ONES_ROWS)
    return pl.pallas_call(
        functools.partial(_vt_expand_kernel, hg=hg, v_dim=v_dim),
        grid=(b, s // ts, n // tr),
        in_specs=[
            pl.BlockSpec((None, tr, f), lambda bi, t, r: (layer, r, 0)),
            pl.BlockSpec((1, ts, f), lambda bi, t, r: (bi, t, 0)),
        ],
        out_specs=pl.BlockSpec((1, rows, ts), lambda bi, t, r: (bi, r, t)),
        out_shape=jax.ShapeDtypeStruct((b, n // tr * rows, s), BF16),
        compiler_params=_cparams("parallel", "parallel", "arbitrary"),
        name="vt_expand",
    )(w_vt, ckr)


def _attn_dec_kernel(q_ref, cp_ref, kp_ref, cn_ref, wuk_ref, wuv_ref, o_ref, cpb, kpb, *, hg, kv_rank, rope, s):
    g = pl.program_id(1)

    @pl.when(g == 0)
    def _():
        cpb[...] = cp_ref[0, 0].astype(BF16)
        kpb[...] = kp_ref[0, 0].astype(BF16)

    nt = (((1,), (1,)), ((), ()))
    ql, qr = [], []
    for hh in range(hg):
        base = hh * HEAD_W
        qn = q_ref[0, :, base:base + LANES]
        ql.append(jnp.dot(qn, wuk_ref[hh], preferred_element_type=F32).astype(BF16))
        qr.append(q_ref[0, :, base + LANES:base + LANES + rope])
    ql = jnp.concatenate(ql, axis=0)
    qr = jnp.concatenate(qr, axis=0)
    c_new = cn_ref[0, :, :kv_rank]
    k_new = cn_ref[0, :, kv_rank:kv_rank + rope]
    c_past = cpb[...]
    s_past = (lax.dot_general(ql, c_past, nt, preferred_element_type=F32)
              + lax.dot_general(qr, kpb[...], nt, preferred_element_type=F32))
    s_new = (lax.dot_general(ql, c_new, nt, preferred_element_type=F32)
             + lax.dot_general(qr, k_new, nt, preferred_element_type=F32))
    m = jnp.maximum(jnp.max(s_past, axis=-1, keepdims=True), jnp.max(s_new, axis=-1, keepdims=True))
    p_past = jnp.exp2(s_past - m)
    p_new = jnp.exp2(s_new - m)
    den = jnp.sum(p_past, axis=-1, keepdims=True) + jnp.sum(p_new, axis=-1, keepdims=True)
    o_lat = (jnp.dot(p_past.astype(BF16), c_past, preferred_element_type=F32)
             + jnp.dot(p_new.astype(BF16), c_new, preferred_element_type=F32)) / den
    o_lat = o_lat.astype(BF16)
    for hh in range(hg):
        oh = jnp.dot(o_lat[hh * s:(hh + 1) * s, :], wuv_ref[hh], preferred_element_type=F32)
        o_ref[0, :, hh * LANES:(hh + 1) * LANES] = oh.astype(o_ref.dtype)


def _attn_dec_call(q_aug, cache_kv, cache_kr, ckr, wuk_t, wuv_h, layer, *, heads, kv_rank, rope, v_dim):
    b, s, _ = q_aug.shape
    past = cache_kv.shape[2]
    assert past % CHUNK == 0 and s <= CHUNK and v_dim == LANES
    hg = _tile(heads, 4)
    kern = functools.partial(_attn_dec_kernel, hg=hg, kv_rank=kv_rank, rope=rope, s=s)
    return pl.pallas_call(
        kern,
        grid=(b, heads // hg),
        in_specs=[
            pl.BlockSpec((1, s, hg * HEAD_W), lambda bi, g: (bi, 0, g)),
            pl.BlockSpec((1, 1, past, kv_rank), lambda bi, g: (layer, bi, 0, 0)),
            pl.BlockSpec((1, 1, past, rope), lambda bi, g: (layer, bi, 0, 0)),
            pl.BlockSpec((1, s, kv_rank + LANES), lambda bi, g: (bi, 0, 0)),
            pl.BlockSpec((None, hg, LANES, kv_rank), lambda bi, g: (layer, g, 0, 0)),
            pl.BlockSpec((None, hg, kv_rank, v_dim), lambda bi, g: (layer, g, 0, 0)),
        ],
        out_specs=pl.BlockSpec((1, s, hg * v_dim), lambda bi, g: (bi, 0, g)),
        out_shape=jax.ShapeDtypeStruct((b, s, heads * v_dim), BF16),
        scratch_shapes=[pltpu.VMEM((past, kv_rank), BF16), pltpu.VMEM((past, rope), BF16)],
        compiler_params=_cparams("parallel", "arbitrary"),
        name="attn_sample",
    )(q_aug, cache_kv, cache_kr, ckr, wuk_t, wuv_h)


def _rotate_half_cols(w, half):
    return jnp.concatenate([-w[..., half:], w[..., :half]], axis=-1)


def _prep_weights(p, dims):
    q_rank, kv_rank, rope, heads, nope, v_dim, d_rnn, d_model = (
        dims[k] for k in ("q_rank", "kv_rank", "rope", "heads", "nope", "v_dim", "d_rnn", "d_model"))
    half = rope // 2
    depth = p["w_in"].shape[0]
    off_kr = q_rank + kv_rank
    off_rx = off_kr + rope
    w_in_t = jnp.swapaxes(p["w_in"], 1, 2)
    w_kr_t = w_in_t[:, off_kr:off_rx]
    w_kr_rot = jnp.concatenate([-w_kr_t[:, half:], w_kr_t[:, :half]], axis=1)
    w_a = jnp.concatenate([w_in_t[:, :off_rx], w_kr_rot], axis=1).astype(BF16)
    w_b = w_in_t[:, off_rx:].astype(BF16)

    wq = p["w_uq"].reshape(depth, q_rank, heads, nope + rope)
    wq_r = wq[..., nope:]
    w_uq = jnp.concatenate([wq[..., :nope], wq_r, _rotate_half_cols(wq_r, half)], axis=-1)
    w_uq = w_uq.reshape(depth, q_rank, heads * HEAD_W).astype(BF16)

    zk = jnp.zeros((depth, kv_rank, heads, HEAD_W - nope), F32)
    k_top = jnp.concatenate([p["w_uk"], zk], axis=-1)
    eye = jnp.zeros((LANES, HEAD_W), F32).at[jnp.arange(rope), nope + jnp.arange(rope)].set(1.0)
    k_bot = jnp.broadcast_to(eye[None, :, None, :], (depth, LANES, heads, HEAD_W))
    k_w = jnp.concatenate([k_top, k_bot], axis=1).reshape(depth, kv_rank + LANES, heads * HEAD_W)
    v_w = jnp.concatenate([p["w_uv"], jnp.zeros((depth, LANES, heads, v_dim), F32)], axis=1)
    w_vt = jnp.transpose(v_w, (0, 2, 3, 1)).reshape(depth, heads * v_dim, kv_rank + LANES).astype(BF16)

    return dict(
        w_a=w_a, w_b=w_b, w_uq=w_uq, w_k=k_w.astype(BF16), w_vt=w_vt,
        wuk_t=jnp.transpose(p["w_uk"], (0, 2, 3, 1)).astype(BF16),
        wuv_h=jnp.transpose(p["w_uv"], (0, 2, 1, 3)).astype(BF16),
        w_o_attn=p["w_o_attn"].astype(BF16), w_o_rnn=p["w_o_rnn"].astype(BF16),
        w_out=p["w_out"].astype(BF16), w_up=p["w_up"].astype(BF16), w_down=p["w_down"].astype(BF16),
        w_rg_a=p["w_rg_a"].astype(BF16), w_rg_x=p["w_rg_x"].astype(BF16),
    )


def _rope_tables(past, s, rope, rows):
    half = rope // 2
    pos = past + jnp.arange(s, dtype=jnp.int32)
    inv = ROPE_THETA ** (-jnp.arange(half, dtype=F32) / half)
    ang = pos.astype(F32)[:, None] * inv[None, :]
    zeros = jnp.zeros((s, LANES - rope), F32)
    cos_t = jnp.concatenate([jnp.cos(ang), jnp.cos(ang), zeros], axis=-1)
    sin_t = jnp.concatenate([jnp.sin(ang), jnp.sin(ang), zeros], axis=-1)
    if rows > s:
        cos_t = jnp.tile(cos_t, (rows // s, 1))
        sin_t = jnp.tile(sin_t, (rows // s, 1))
    return cos_t, sin_t


def _run_trunk(x, mod, p, w, dims, *, cache_kv=None, cache_kr=None, h_past=None, conv_past=None):
    b, s, d = x.shape
    m = b * s
    depth = p["w_in"].shape[0]
    q_rank, kv_rank, rope, heads, nope, v_dim, d_rnn = (
        dims[k] for k in ("q_rank", "kv_rank", "rope", "heads", "nope", "v_dim", "d_rnn"))
    alpha = (2 * depth) ** 0.25
    scale = (nope + rope) ** -0.5 * math.log2(math.e)
    has_past = cache_kv is not None
    past = cache_kv.shape[2] if has_past else 0
    hist = p["conv_w"].shape[1] - 1
    if not has_past:
        h_past = jnp.zeros((1, b, 1, d_rnn), F32)
        conv_past = jnp.zeros((1, b, hist, d_rnn), F32)
    else:
        h_past = h_past.reshape(depth, b, 1, d_rnn)

    tm = _tile(m, 1024)
    assert tm % s == 0 or s % tm == 0
    cos_t, sin_t = _rope_tables(past, s, rope, max(s, tm))
    nt = cos_t.shape[0] // tm
    tab_spec = lambda arr: (arr, (tm, LANES), lambda i, j, k: (i % nt, 0))

    ckv_l, kr_l, h_l, conv_l = [], [], [], []
    x, h = _ln_mod_call(x, p["ln_in_g"], p["ln_in_b"], nmod=mod, mod_layer=0, sc_row=1, sh_row=0)
    tm_a = _tile(m, 512)
    for l in range(depth):
        h2 = h.reshape(m, d)
        q_c, ckv, kr, ckr = _inproj_a_call(
            h2, w["w_a"], l, p["q_norm_g"][l].reshape(1, q_rank), p["kv_norm_g"][l].reshape(1, kv_rank),
            cos_t, sin_t, q_rank=q_rank, kv_rank=kv_rank, rope=rope, tm=tm_a)
        n_b = w["w_b"].shape[1]
        proj_b = _inproj_b_call(h2, w["w_b"], l, d_rnn=d_rnn)
        q_aug = _matmul(
            q_c, w["w_uq"], l, out_dtype=BF16, name="uq", extra=[tab_spec(cos_t), tab_spec(sin_t)],
            epilogue=functools.partial(_ep_uq, rope=rope, scale=scale))
        q3 = q_aug.reshape(b, s, heads * HEAD_W)
        if has_past:
            o = _attn_dec_call(q3, cache_kv, cache_kr, ckr.reshape(b, s, kv_rank + LANES),
                               w["wuk_t"], w["wuv_h"], l,
                               heads=heads, kv_rank=kv_rank, rope=rope, v_dim=v_dim)
        else:
            k_aug = _matmul(ckr, w["w_k"], l, out_dtype=BF16, name="k_expand", epilogue=_ep_store)
            vt = _vt_expand_call(ckr.reshape(b, s, kv_rank + LANES), w["w_vt"], l, v_dim=v_dim)
            o = _attn_call(q3, k_aug.reshape(b, s, heads * HEAD_W), vt, heads=heads, v_dim=v_dim)
        hb, h_last, conv_n = _rnn_call(
            proj_b.reshape(b, s, n_b), p["conv_w"][l], p["conv_b"][l], w["w_rg_a"], w["w_rg_x"], l,
            p["b_rg_a"][l].reshape(-1), p["b_rg_x"][l].reshape(-1), p["rg_lambda"][l],
            h_past, conv_past, l if has_past else 0, d_rnn=d_rnn, xr_col=0, gy_col=d_rnn)
        merged = _dual_call(o.reshape(m, heads * v_dim), hb.reshape(m, d_rnn), w["w_o_attn"], w["w_o_rnn"], l,
                            proj_b, 2 * d_rnn, 2 * d_rnn + d)
        mix = _matmul(merged, w["w_out"], l, out_dtype=BF16, name="w_out", epilogue=_ep_store)
        x, h = _ln_mod_call(x, p["ln_mix_g"][l], p["ln_mix_b"][l], res=mix.reshape(b, s, d), gmod=mod,
                            gate_layer=l, gate_row=2, nmod=mod, mod_layer=l, sc_row=4, sh_row=3, alpha=alpha)
        up = _matmul(h.reshape(m, d), w["w_up"], l, out_dtype=BF16, name="ffn_up", epilogue=_ep_relu2)
        ff = _matmul(up, w["w_down"], l, out_dtype=BF16, tk=4096, name="ffn_down", epilogue=_ep_store)
        last = l == depth - 1
        x, h = _ln_mod_call(x, p["ln_ffn_g"][l], p["ln_ffn_b"][l], res=ff.reshape(b, s, d), gmod=mod,
                            gate_layer=l, gate_row=5, nmod=None if last else mod,
                            mod_layer=0 if last else l + 1, sc_row=1, sh_row=0, alpha=alpha)
        ckv_l.append(ckv.reshape(b, s, kv_rank))
        kr_l.append(kr.reshape(b, s, rope))
        h_l.append(h_last.reshape(b, d_rnn))
        conv_l.append(conv_n)
    return x, jnp.stack(ckv_l), jnp.stack(kr_l), jnp.stack(h_l), jnp.stack(conv_l)


def kernel(x_prompt, x_sample, cache_kv_latent, cache_k_rope, state_rglru_h, state_conv, c_prompt, c_sample, ln_in_g, ln_in_b, w_in, q_norm_g, w_uq, kv_norm_g, w_uk, w_uv, w_o_attn, conv_w, conv_b, w_rg_a, b_rg_a, w_rg_x, b_rg_x, rg_lambda, w_o_rnn, w_out, w_mod, b_mod, ln_mix_g, ln_mix_b, w_up, w_down, ln_ffn_g, ln_ffn_b):
    p = dict(ln_in_g=ln_in_g, ln_in_b=ln_in_b, w_in=w_in, q_norm_g=q_norm_g, w_uq=w_uq, kv_norm_g=kv_norm_g,
             w_uk=w_uk, w_uv=w_uv, w_o_attn=w_o_attn, conv_w=conv_w, conv_b=conv_b, w_rg_a=w_rg_a,
             b_rg_a=b_rg_a, w_rg_x=w_rg_x, b_rg_x=b_rg_x, rg_lambda=rg_lambda, w_o_rnn=w_o_rnn, w_out=w_out,
             w_mod=w_mod, b_mod=b_mod, ln_mix_g=ln_mix_g, ln_mix_b=ln_mix_b, w_up=w_up, w_down=w_down,
             ln_ffn_g=ln_ffn_g, ln_ffn_b=ln_ffn_b)
    depth, d_model, _ = w_in.shape
    _, kv_rank, heads, nope = w_uk.shape
    dims = dict(d_model=d_model, q_rank=q_norm_g.shape[-1], kv_rank=kv_rank, heads=heads, nope=nope,
                rope=w_uq.shape[-1] // heads - nope, v_dim=w_uv.shape[-1], d_rnn=conv_w.shape[-1])
    assert dims["nope"] == LANES and dims["rope"] == LANES // 2 and dims["v_dim"] == LANES
    assert dims["d_rnn"] == d_model
    n_mod = w_mod.shape[-1] // d_model

    bp, bs = x_prompt.shape[0], x_sample.shape[0]
    rows = -(-(bp + bs) // SUBLANES) * SUBLANES
    c_all = jnp.concatenate([c_prompt, c_sample, jnp.zeros((rows - bp - bs, d_model), F32)], axis=0)
    mod_all = _mod_call(c_all, w_mod, b_mod).reshape(depth, rows, n_mod, d_model)
    mod_p = mod_all[:, :bp]
    mod_s = mod_all[:, bp:bp + bs]

    w = _prep_weights(p, dims)
    y_p, ckv_p, kr_p, h_p, conv_p = _run_trunk(x_prompt, mod_p, p, w, dims)
    y_s, ckv_s, kr_s, h_s, conv_s = _run_trunk(
        x_sample, mod_s, p, w, dims, cache_kv=cache_kv_latent, cache_kr=cache_k_rope,
        h_past=state_rglru_h, conv_past=state_conv)
    return (y_p, y_s, ckv_p, kr_p, h_p, conv_p, ckv_s, kr_s, h_s, conv_s)
```
